```python
import math, functools
import jax, jax.numpy as jnp
from jax import lax
import numpy as np

D_MODEL = 1024
BATCH = 2
SEQ = 8192
DEPTH = 2
DEC_BATCH = 128
DEC_SEQ = 8
PAST_LEN = 8192
PAGE_SIZE = 128

MLA_HEADS = 8
MLA_NOPE_DIM = 64
MLA_ROPE_DIM = 32
MLA_V_DIM = 64
MLA_Q_RANK = 256
MLA_KV_RANK = 128
MLA_SCALE = (MLA_NOPE_DIM + MLA_ROPE_DIM) ** -0.5
ROPE_BASE = 10000.0
FOX_HEADS = 8
FOX_HEAD_DIM = 64
FOX_WIDTH = FOX_HEADS * FOX_HEAD_DIM
FOX_SCALE = FOX_HEAD_DIM ** -0.5
FORGET_BIAS_INIT = 3.0
MIX_WIDTH = MLA_HEADS * MLA_V_DIM + FOX_WIDTH
_IN_SIZES = (MLA_Q_RANK, MLA_KV_RANK, MLA_ROPE_DIM, FOX_WIDTH, FOX_WIDTH, FOX_WIDTH, FOX_HEADS)
IN_WIDTH = sum(_IN_SIZES)
IN_SPLIT_POINTS = tuple(int(v) for v in np.cumsum(_IN_SIZES)[:-1])
MEM_TOKENS = 256
MEM_HEADS = 4
MEM_HEAD_DIM = D_MODEL // MEM_HEADS
MEM_SCALE = MEM_HEAD_DIM ** -0.5
D_FF = 2816
CONV_WIDTH = 3
Q_BLOCK = 128
LN_EPS = 1e-5
RMS_EPS = 1e-6
DEEPNORM_ALPHA = (2 * DEPTH) ** 0.25
DEEPNORM_BETA = (8 * DEPTH) ** -0.25

kernel_name = "hymba_mla_fox_convffn_decoder_step"


def _layernorm(x, g, b):
    xf = x.astype(jnp.float32)
    mu = jnp.mean(xf, axis=-1, keepdims=True)
    var = jnp.mean(jnp.square(xf - mu), axis=-1, keepdims=True)
    return ((xf - mu) * lax.rsqrt(var + LN_EPS) * g + b).astype(x.dtype)


def _rmsnorm(x, g):
    xf = x.astype(jnp.float32)
    return (xf * lax.rsqrt(jnp.mean(xf * xf, axis=-1, keepdims=True) + RMS_EPS) * g).astype(x.dtype)


def _rope(x, pos):
    half = MLA_ROPE_DIM // 2
    inv = ROPE_BASE ** (-jnp.arange(half, dtype=jnp.float32) / half)
    ang = pos.astype(jnp.float32)[:, None] * inv[None, :]
    ang = ang.reshape(ang.shape[0], *([1] * (x.ndim - 3)), half)
    cos, sin = jnp.cos(ang), jnp.sin(ang)
    x1 = x[..., :half].astype(jnp.float32)
    x2 = x[..., half:].astype(jnp.float32)
    return jnp.concatenate([x1 * cos - x2 * sin, x2 * cos + x1 * sin], axis=-1).astype(x.dtype)


def _sweep_queries(attend, q_arrays, q_pos):
    n_q = q_pos.shape[0]
    blk = min(Q_BLOCK, n_q)
    n_blk = n_q // blk
    blocks = tuple(jnp.moveaxis(a.reshape(a.shape[0], n_blk, blk, *a.shape[2:]), 1, 0) for a in q_arrays)
    out = lax.map(lambda xs: attend(*xs), (blocks, q_pos.reshape(n_blk, blk)))
    out = jnp.moveaxis(out, 0, 1)
    return out.reshape(out.shape[0], n_q, *out.shape[3:])


def _softmax_over_segments(scores, masks):
    s = jnp.concatenate([jnp.where(m, sc, -jnp.inf) for sc, m in zip(scores, masks)], axis=-1)
    p = jax.nn.softmax(s, axis=-1)
    bounds = [int(v) for v in np.cumsum([sc.shape[-1] for sc in scores])[:-1]]
    return jnp.split(p, bounds, axis=-1)


def _mla_attention(q_abs, q_rope, q_pos, segments):
    def attend(qb, pb):
        ql, qr = qb
        scores = [(jnp.einsum('bqhr,bsr->bhqs', ql, ckv) + jnp.einsum('bqhd,bsd->bhqs', qr, kr)).astype(jnp.float32) * MLA_SCALE
                  for ckv, kr, _ in segments]
        masks = [kp[None, :] <= pb[:, None] for _, _, kp in segments]
        probs = _softmax_over_segments(scores, masks)
        outs = [jnp.einsum('bhqs,bsr->bqhr', p.astype(seg[0].dtype), seg[0]) for p, seg in zip(probs, segments)]
        return functools.reduce(jnp.add, outs)
    return _sweep_queries(attend, (q_abs, q_rope), q_pos)


def _fox_attention(q, c_q, q_pos, segments):
    def attend(qb, pb):
        qq, cq = qb
        cq = jnp.swapaxes(cq, 1, 2)[..., None]
        scores = [jnp.einsum('bqhd,bshd->bhqs', qq, k).astype(jnp.float32) * FOX_SCALE
                  + (cq - jnp.swapaxes(ck, 1, 2)[:, :, None, :])
                  for k, _, ck, _ in segments]
        masks = [kp[None, :] <= pb[:, None] for _, _, _, kp in segments]
        probs = _softmax_over_segments(scores, masks)
        outs = [jnp.einsum('bhqs,bshd->bqhd', p.astype(seg[1].dtype), seg[1]) for p, seg in zip(probs, segments)]
        return functools.reduce(jnp.add, outs)
    return _sweep_queries(attend, (q, c_q), q_pos)


def _gather_pages(pool, layer, page_table):
    rows = pool[layer, page_table]
    return rows.reshape(rows.shape[0], rows.shape[1] * rows.shape[2], *rows.shape[3:])


def _layer(x, pos, mem_k, mem_v, conv_prefix, past, w):
    n_b, n_t, _ = x.shape
    h = x @ w['w_in']
    q_in, kv_in, kr_in, fq, fk, fv, f_logit = jnp.split(h, IN_SPLIT_POINTS, axis=-1)
    q = jnp.einsum('btr,rhe->bthe', _rmsnorm(q_in, w['g_q_lat']), w['w_uq'])
    q_nope = q[..., :MLA_NOPE_DIM]
    q_rope = _rope(q[..., MLA_NOPE_DIM:], pos)
    q_abs = jnp.einsum('bthn,rhn->bthr', q_nope, w['w_uk'])
    ckv = _rmsnorm(kv_in, w['g_kv_lat'])
    krope = _rope(kr_in, pos)
    fq = fq.reshape(n_b, n_t, FOX_HEADS, FOX_HEAD_DIM)
    fk = fk.reshape(n_b, n_t, FOX_HEADS, FOX_HEAD_DIM)
    fv = fv.reshape(n_b, n_t, FOX_HEADS, FOX_HEAD_DIM)
    logf = jax.nn.log_sigmoid((f_logit + w['b_forget']).astype(jnp.float32))
    c_new = jnp.cumsum(logf, axis=1)
    if past is None:
        mla_segs = [(ckv, krope, pos)]
        fox_segs = [(fk, fv, c_new, pos)]
    else:
        p_ckv, p_kr, p_fk, p_fv, p_logf = past
        p_pos = jnp.arange(p_ckv.shape[1], dtype=jnp.int32)
        c_past = jnp.cumsum(p_logf.astype(jnp.float32), axis=1)
        c_new = c_past[:, -1:, :] + c_new
        mla_segs = [(p_ckv, p_kr, p_pos), (ckv, krope, pos)]
        fox_segs = [(p_fk, p_fv, c_past, p_pos), (fk, fv, c_new, pos)]
    o_lat = _mla_attention(q_abs, q_rope, pos, mla_segs)
    o_mla = jnp.einsum('bthr,rhv->bthv', o_lat, w['w_uv']).reshape(n_b, n_t, MLA_HEADS * MLA_V_DIM)
    o_fox = _fox_attention(fq, c_new, pos, fox_segs).reshape(n_b, n_t, FOX_WIDTH)
    mix = jnp.concatenate([_rmsnorm(o_mla, w['g_mla_out']), _rmsnorm(o_fox, w['g_fox_out'])], axis=-1) @ w['w_o']
    x = _layernorm(DEEPNORM_ALPHA * x + mix, w['ln1_g'], w['ln1_b'])
    mq = jnp.einsum('btd,dhe->bthe', x, w['w_mq'])
    ms = jnp.einsum('bthe,bmhe->bhtm', mq, mem_k).astype(jnp.float32) * MEM_SCALE
    mp = jax.nn.softmax(ms, axis=-1)
    mo = jnp.einsum('bhtm,bmhe->bthe', mp.astype(mem_v.dtype), mem_v)
    cross = jnp.einsum('bthe,hed->btd', mo, w['w_mo'])
    x = _layernorm(DEEPNORM_ALPHA * x + cross, w['ln2_g'], w['ln2_b'])
    u = x @ w['w_up']
    ext = jnp.concatenate([conv_prefix.astype(u.dtype), u], axis=1)
    y = w['conv_b'] + functools.reduce(jnp.add, [w['conv_w'][k] * ext[:, k:k + n_t] for k in range(CONV_WIDTH)])
    val, gate = jnp.split(y, 2, axis=-1)
    ffn = (jax.nn.silu(gate) * val) @ w['w_down']
    x = _layernorm(DEEPNORM_ALPHA * x + ffn, w['ln3_g'], w['ln3_b'])
    new_conv = ext[:, ext.shape[1] - (CONV_WIDTH - 1):]
    return x, (ckv, krope, fk, fv, logf.astype(x.dtype), new_conv)


def setup_inputs(seed: int = 0) -> dict:
    key = jax.random.key(seed)
    keys = iter(jax.random.split(key, 64))

    def nrm(shape, scale=1.0):
        return jax.random.normal(next(keys), shape, jnp.float32) * scale

    n_pages = PAST_LEN // PAGE_SIZE
    n_used = DEC_BATCH * n_pages
    n_pool = n_used + (n_used + 3) // 4
    page_table = jax.random.permutation(next(keys), n_pool)[:n_used].reshape(DEC_BATCH, n_pages).astype(jnp.int32)
    f2 = 2 * D_FF
    return {
        'x_prompt': nrm((BATCH, SEQ, D_MODEL)),
        'x_sample': nrm((DEC_BATCH, DEC_SEQ, D_MODEL)),
        'mem_prompt': nrm((BATCH, MEM_TOKENS, D_MODEL)),
        'cache_mla_ckv': nrm((DEPTH, n_pool, PAGE_SIZE, MLA_KV_RANK)),
        'cache_mla_krope': nrm((DEPTH, n_pool, PAGE_SIZE, MLA_ROPE_DIM)),
        'cache_fox_k': nrm((DEPTH, n_pool, PAGE_SIZE, FOX_HEADS, FOX_HEAD_DIM)),
        'cache_fox_v': nrm((DEPTH, n_pool, PAGE_SIZE, FOX_HEADS, FOX_HEAD_DIM)),
        'cache_fox_logf': jax.nn.log_sigmoid(FORGET_BIAS_INIT + nrm((DEPTH, n_pool, PAGE_SIZE, FOX_HEADS))),
        'cache_mem_k': nrm((DEPTH, DEC_BATCH, MEM_TOKENS, MEM_HEADS, MEM_HEAD_DIM)),
        'cache_mem_v': nrm((DEPTH, DEC_BATCH, MEM_TOKENS, MEM_HEADS, MEM_HEAD_DIM)),
        'state_conv': nrm((DEPTH, DEC_BATCH, CONV_WIDTH - 1, f2)),
        'page_table': page_table,
        'w_in': nrm((DEPTH, D_MODEL, IN_WIDTH), D_MODEL ** -0.5),
        'b_forget': FORGET_BIAS_INIT + nrm((DEPTH, FOX_HEADS), 0.1),
        'g_q_lat': 1.0 + nrm((DEPTH, MLA_Q_RANK), 0.02),
        'g_kv_lat': 1.0 + nrm((DEPTH, MLA_KV_RANK), 0.02),
        'w_uq': nrm((DEPTH, MLA_Q_RANK, MLA_HEADS, MLA_NOPE_DIM + MLA_ROPE_DIM), MLA_Q_RANK ** -0.5),
        'w_uk': nrm((DEPTH, MLA_KV_RANK, MLA_HEADS, MLA_NOPE_DIM), MLA_KV_RANK ** -0.5),
        'w_uv': nrm((DEPTH, MLA_KV_RANK, MLA_HEADS, MLA_V_DIM), MLA_KV_RANK ** -0.5),
        'g_mla_out': 1.0 + nrm((DEPTH, MLA_HEADS * MLA_V_DIM), 0.02),
        'g_fox_out': 1.0 + nrm((DEPTH, FOX_WIDTH), 0.02),
        'w_o': nrm((DEPTH, MIX_WIDTH, D_MODEL), MIX_WIDTH ** -0.5 * DEEPNORM_BETA),
        'ln1_g': 1.0 + nrm((DEPTH, D_MODEL), 0.02),
        'ln1_b': nrm((DEPTH, D_MODEL), 0.02),
        'w_mq': nrm((DEPTH, D_MODEL, MEM_HEADS, MEM_HEAD_DIM), D_MODEL ** -0.5),
        'w_mk': nrm((DEPTH, D_MODEL, MEM_HEADS, MEM_HEAD_DIM), D_MODEL ** -0.5),
        'w_mv': nrm((DEPTH, D_MODEL, MEM_HEADS, MEM_HEAD_DIM), D_MODEL ** -0.5),
        'w_mo': nrm((DEPTH, MEM_HEADS, MEM_HEAD_DIM, D_MODEL), D_MODEL ** -0.5 * DEEPNORM_BETA),
        'ln2_g': 1.0 + nrm((DEPTH, D_MODEL), 0.02),
        'ln2_b': nrm((DEPTH, D_MODEL), 0.02),
        'w_up': nrm((DEPTH, D_MODEL, f2), D_MODEL ** -0.5),
        'conv_w': nrm((DEPTH, CONV_WIDTH, f2), CONV_WIDTH ** -0.5),
        'conv_b': nrm((DEPTH, f2), 0.02),
        'w_down': nrm((DEPTH, D_FF, D_MODEL), D_FF ** -0.5 * DEEPNORM_BETA),
        'ln3_g': 1.0 + nrm((DEPTH, D_MODEL), 0.02),
        'ln3_b': nrm((DEPTH, D_MODEL), 0.02),
    }


def reference(x_prompt, x_sample, mem_prompt, cache_mla_ckv, cache_mla_krope, cache_fox_k, cache_fox_v, cache_fox_logf,
              cache_mem_k, cache_mem_v, state_conv, page_table, w_in, b_forget, g_q_lat, g_kv_lat, w_uq, w_uk, w_uv,
              g_mla_out, g_fox_out, w_o, ln1_g, ln1_b, w_mq, w_mk, w_mv, w_mo, ln2_g, ln2_b, w_up, conv_w, conv_b,
              w_down, ln3_g, ln3_b):
    n_past = page_table.shape[1] * PAGE_SIZE
    pos_p = jnp.arange(x_prompt.shape[1], dtype=jnp.int32)
    pos_s = n_past + jnp.arange(x_sample.shape[1], dtype=jnp.int32)
    conv_zero = jnp.zeros((x_prompt.shape[0], CONV_WIDTH - 1, 2 * D_FF), x_prompt.dtype)
    xp, xs = x_prompt, x_sample
    p_ckv, p_kr, p_fk, p_fv, p_lf, p_mk, p_mv, p_cv = [], [], [], [], [], [], [], []
    s_ckv, s_kr, s_fk, s_fv, s_lf, s_cv = [], [], [], [], [], []
    for l in range(DEPTH):
        w = dict(w_in=w_in[l], b_forget=b_forget[l], g_q_lat=g_q_lat[l], g_kv_lat=g_kv_lat[l], w_uq=w_uq[l],
                 w_uk=w_uk[l], w_uv=w_uv[l], g_mla_out=g_mla_out[l], g_fox_out=g_fox_out[l], w_o=w_o[l],
                 ln1_g=ln1_g[l], ln1_b=ln1_b[l], w_mq=w_mq[l], w_mo=w_mo[l], ln2_g=ln2_g[l], ln2_b=ln2_b[l],
                 w_up=w_up[l], conv_w=conv_w[l], conv_b=conv_b[l], w_down=w_down[l], ln3_g=ln3_g[l], ln3_b=ln3_b[l])
        mk = jnp.einsum('bmd,dhe->bmhe', mem_prompt, w_mk[l])
        mv = jnp.einsum('bmd,dhe->bmhe', mem_prompt, w_mv[l])
        xp, st = _layer(xp, pos_p, mk, mv, conv_zero, None, w)
        for lst, val in zip((p_ckv, p_kr, p_fk, p_fv, p_lf, p_cv), st):
            lst.append(val)
        p_mk.append(mk)
        p_mv.append(mv)
        past = (_gather_pages(cache_mla_ckv, l, page_table), _gather_pages(cache_mla_krope, l, page_table),
                _gather_pages(cache_fox_k, l, page_table), _gather_pages(cache_fox_v, l, page_table),
                _gather_pages(cache_fox_logf, l, page_table))
        xs, st = _layer(xs, pos_s, cache_mem_k[l], cache_mem_v[l], state_conv[l], past, w)
        for lst, val in zip((s_ckv, s_kr, s_fk, s_fv, s_lf, s_cv), st):
            lst.append(val)
    return (xp, xs,
            jnp.stack(p_ckv), jnp.stack(p_kr), jnp.stack(p_fk), jnp.stack(p_fv), jnp.stack(p_lf),
            jnp.stack(p_mk), jnp.stack(p_mv), jnp.stack(p_cv),
            jnp.stack(s_ckv), jnp.stack(s_kr), jnp.stack(s_fk), jnp.stack(s_fv), jnp.stack(s_lf), jnp.stack(s_cv))
```

```python
import functools

import numpy as np
import jax
import jax.numpy as jnp
from jax import lax
from jax.experimental import pallas as pl
from jax.experimental.pallas import tpu as pltpu

F32 = jnp.float32
BF16 = jnp.bfloat16

DEPTH = 2
PAGE = 128
H_MLA = 8
NOPE = 64
ROPE = 32
V_DIM = 64
Q_RANK = 256
KV_RANK = 128
MLA_SCALE = (NOPE + ROPE) ** -0.5
ROPE_BASE = 10000.0
H_FOX = 8
D_FOX = 64
FOX_W = H_FOX * D_FOX
FOX_SCALE = D_FOX ** -0.5
MEM_TOK = 256
H_MEM = 4
D_MEM = 256
MEM_SCALE = D_MEM ** -0.5
LN_EPS = 1e-5
RMS_EPS = 1e-6
ALPHA = (2 * DEPTH) ** 0.25

LANES = 128
SUBLANES = 8
VMEM_LIMIT = 56 * 1024 * 1024

C_Q, C_KV, C_KR, C_KS, C_FQ, C_FK, C_FV, C_FL, C_END = 0, 256, 384, 512, 640, 1152, 1664, 2176, 2304

NEG_INF = float("-inf")


def _mm(a, b):
    return jnp.dot(a, b, preferred_element_type=F32)


def _mm_nt(a, b):
    return lax.dot_general(a, b, (((1,), (1,)), ((), ())), preferred_element_type=F32)


def _mm_tn(a, b):
    return lax.dot_general(a, b, (((0,), (0,)), ((), ())), preferred_element_type=F32)


def _rms(x, g):
    return x * lax.rsqrt(jnp.mean(x * x, axis=-1, keepdims=True) + RMS_EPS) * g


def _ln(x, g, b):
    mu = jnp.mean(x, axis=-1, keepdims=True)
    xc = x - mu
    var = jnp.mean(xc * xc, axis=-1, keepdims=True)
    return xc * lax.rsqrt(var + LN_EPS) * g + b


def _split3(x):
    hi = x.astype(BF16)
    r = x - hi.astype(F32)
    mid = r.astype(BF16)
    lo = (r - mid.astype(F32)).astype(BF16)
    return hi, mid, lo


def _const_spec(shape):
    nd = len(shape)
    return pl.BlockSpec(shape, lambda *_: (0,) * nd)


def _params(sem):
    return pltpu.CompilerParams(dimension_semantics=sem, vmem_limit_bytes=VMEM_LIMIT)


def _inproj_kernel(x_ref, pos_ref, w_ref, bf_ref, gq_ref, gkv_ref, wuq_ref, wuk_ref, inv_ref, sgn_ref,
                   qcat_ref, kcat_ref, ckv_ref, kr_ref, fq_ref, fk_ref, fv_ref, lf_ref):
    xb = x_ref[0].astype(BF16)
    h = _mm(xb, w_ref[...])
    ang = pos_ref[0] * inv_ref[...]
    cos = jnp.cos(ang)
    sin = jnp.sin(ang) * sgn_ref[...]
    qn = _rms(h[:, C_Q:C_KV], gq_ref[...]).astype(BF16)
    qu = _mm(qn, wuq_ref[...])
    ckv = _rms(h[:, C_KV:C_KR], gkv_ref[...])
    kr = h[:, C_KR:C_KS] * cos + h[:, C_KS:C_FQ] * sin
    ckv_ref[0] = ckv
    kr_ref[0] = kr[:, :ROPE]
    kcat_ref[0] = jnp.concatenate([ckv.astype(BF16), kr.astype(BF16)], axis=1)
    lane = lax.broadcasted_iota(jnp.int32, (1, LANES), 1)
    slabs = []
    for g in range(2):
        lo = 512 + g * LANES
        slabs.append(qu[:, lo:lo + LANES] * cos + qu[:, lo + 256:lo + 256 + LANES] * sin)
    for hd in range(H_MLA):
        g2 = hd // 2
        qabs = _mm(qu[:, g2 * LANES:(g2 + 1) * LANES].astype(BF16), wuk_ref[hd])
        slab = jnp.where(lane // ROPE == hd % 4, slabs[hd // 4], 0.0)
        qcat_ref[0, hd] = jnp.concatenate(
            [(qabs * MLA_SCALE).astype(BF16), (slab * MLA_SCALE).astype(BF16)], axis=1)
    fq_ref[0] = (h[:, C_FQ:C_FK] * FOX_SCALE).astype(BF16)
    fk_ref[0] = h[:, C_FK:C_FV]
    fv_ref[0] = h[:, C_FV:C_FL]
    z = h[:, C_FL:C_END] + bf_ref[...]
    logf = jnp.minimum(z, 0.0) - jnp.log1p(jnp.exp(-jnp.abs(z)))
    lf_ref[0] = logf[:, :H_FOX]


def _inproj(x, pos, lw, tm):
    b, t, d = x.shape
    grid = (b, t // tm)
    tok = lambda w: pl.BlockSpec((1, tm, w), lambda i, j: (i, j, 0))
    out_shape = (
        jax.ShapeDtypeStruct((b, H_MLA, t, 256), BF16),
        jax.ShapeDtypeStruct((b, t, 256), BF16),
        jax.ShapeDtypeStruct((b, t, KV_RANK), F32),
        jax.ShapeDtypeStruct((b, t, ROPE), F32),
        jax.ShapeDtypeStruct((b, t, FOX_W), BF16),
        jax.ShapeDtypeStruct((b, t, FOX_W), F32),
        jax.ShapeDtypeStruct((b, t, FOX_W), F32),
        jax.ShapeDtypeStruct((b, t, H_FOX), F32),
    )
    out_specs = (
        pl.BlockSpec((1, H_MLA, tm, 256), lambda i, j: (i, 0, j, 0)),
        tok(256), tok(KV_RANK), tok(ROPE), tok(FOX_W), tok(FOX_W), tok(FOX_W), tok(H_FOX),
    )
    in_specs = [
        tok(d), tok(1),
        _const_spec(lw["w_in"].shape), _const_spec((1, LANES)), _const_spec((1, Q_RANK)),
        _const_spec((1, KV_RANK)), _const_spec(lw["w_uq"].shape), _const_spec(lw["w_uk"].shape),
        _const_spec((1, LANES)), _const_spec((1, LANES)),
    ]
    return pl.pallas_call(
        _inproj_kernel, grid=grid, in_specs=in_specs, out_specs=out_specs, out_shape=out_shape,
        compiler_params=_params(("parallel", "parallel")), name="inproj",
    )(x, pos, lw["w_in"], lw["b_forget"], lw["g_q"], lw["g_kv"], lw["w_uq"], lw["w_uk"],
      lw["inv_freq"], lw["rope_sign"])


def _foxprep_kernel(fq_ref, fk_ref, fv_ref, lf_ref, tri_ref, pq_ref, pk_ref, oq_ref, ok_ref,
                    qcat_ref, kcat_ref, vb_ref, carry_ref):
    tm = lf_ref.shape[1]

    @pl.when(pl.program_id(1) == 0)
    def _():
        carry_ref[...] = jnp.zeros_like(carry_ref)

    tri = tri_ref[...]
    c = carry_ref[...]
    for part in _split3(lf_ref[0]):
        c = c + _mm(tri, part)
    carry_ref[...] = c[tm - 1:tm, :]
    parts = _split3(c)
    augq = oq_ref[...]
    augk = ok_ref[...]
    for j in range(3):
        augq = augq + _mm(parts[j], pq_ref[j])
        augk = augk + _mm(parts[j], pk_ref[j])
    augk = augk.astype(BF16)
    lane = lax.broadcasted_iota(jnp.int32, (1, LANES), 1)
    fq = fq_ref[0]
    fkb = fk_ref[0].astype(BF16)
    fvb = fv_ref[0].astype(BF16)
    for g in range(H_FOX // 2):
        kcat_ref[0, g] = jnp.concatenate([fkb[:, g * LANES:(g + 1) * LANES], augk], axis=1)
        vb_ref[0, g] = fvb[:, g * LANES:(g + 1) * LANES]
    for hd in range(H_FOX):
        g = hd // 2
        qs = jnp.where(lane // D_FOX == hd % 2, fq[:, g * LANES:(g + 1) * LANES].astype(F32), 0.0)
        aq = jnp.where(lane // 8 == hd, augq, 0.0)
        qcat_ref[0, hd] = jnp.concatenate([qs.astype(BF16), aq.astype(BF16)], axis=1)


def _foxprep(fq, fk, fv, lf, consts, tm):
    b, t, _ = fq.shape
    tok = lambda w: pl.BlockSpec((1, tm, w), lambda i, j: (i, j, 0))
    out_shape = (
        jax.ShapeDtypeStruct((b, H_FOX, t, 256), BF16),
        jax.ShapeDtypeStruct((b, H_FOX // 2, t, 256), BF16),
        jax.ShapeDtypeStruct((b, H_FOX // 2, t, LANES), BF16),
    )
    out_specs = (
        pl.BlockSpec((1, H_FOX, tm, 256), lambda i, j: (i, 0, j, 0)),
        pl.BlockSpec((1, H_FOX // 2, tm, 256), lambda i, j: (i, 0, j, 0)),
        pl.BlockSpec((1, H_FOX // 2, tm, LANES), lambda i, j: (i, 0, j, 0)),
    )
    in_specs = [tok(FOX_W), tok(FOX_W), tok(FOX_W), tok(H_FOX),
                _const_spec((tm, tm)), _const_spec((3, H_FOX, LANES)), _const_spec((3, H_FOX, LANES)),
                _const_spec((1, LANES)), _const_spec((1, LANES))]
    return pl.pallas_call(
        _foxprep_kernel, grid=(b, t // tm), in_specs=in_specs, out_specs=out_specs, out_shape=out_shape,
        scratch_shapes=[pltpu.VMEM((1, H_FOX), F32)],
        compiler_params=_params(("parallel", "arbitrary")), name="foxprep",
    )(fq, fk, fv, lf, consts["tri"], consts["place_q"], consts["place_k"], consts["ones_q"], consts["ones_k"])


def _tile_pairs(t, tq, tk):
    qi, kj = [], []
    for i in range(t // tq):
        for j in range(((i + 1) * tq - 1) // tk + 1):
            qi.append(i)
            kj.append(j)
    return jnp.asarray(np.array(qi, np.int32)), jnp.asarray(np.array(kj, np.int32))


def _mla_flash_kernel(qi_ref, kj_ref, q_ref, k_ref, o_ref, m_ref, l_ref, acc_ref, *, tq, tk):
    p = pl.program_id(1)
    qi = qi_ref[p]
    kj = kj_ref[p]
    rows = H_MLA * tq

    @pl.when(kj == 0)
    def _():
        m_ref[...] = jnp.full_like(m_ref, NEG_INF)
        l_ref[...] = jnp.zeros_like(l_ref)
        acc_ref[...] = jnp.zeros_like(acc_ref)

    def update(masked):
        k = k_ref[0]
        s = _mm_nt(q_ref[0].reshape(rows, 256), k)
        if masked:
            row = qi * tq + lax.broadcasted_iota(jnp.int32, (rows, tk), 0) % tq
            col = kj * tk + lax.broadcasted_iota(jnp.int32, (rows, tk), 1)
            s = jnp.where(col <= row, s, NEG_INF)
        m_old = m_ref[...]
        m_new = jnp.maximum(m_old, jnp.max(s, axis=1, keepdims=True))
        alpha = jnp.exp(m_old - m_new)
        pr = jnp.exp(s - m_new)
        l_ref[...] = alpha * l_ref[...] + jnp.sum(pr, axis=1, keepdims=True)
        acc_ref[...] = alpha * acc_ref[...] + _mm(pr.astype(BF16), k[:, :KV_RANK])
        m_ref[...] = m_new

    straddle = kj * tk + tk - 1 > qi * tq
    pl.when(straddle)(functools.partial(update, True))
    pl.when(jnp.logical_not(straddle))(functools.partial(update, False))

    @pl.when((kj + 1) * tk >= (qi + 1) * tq)
    def _():
        o = acc_ref[...] / l_ref[...]
        for hd in range(H_MLA):
            o_ref[0, :, hd * KV_RANK:(hd + 1) * KV_RANK] = o[hd * tq:(hd + 1) * tq]


def _mla_flash(qcat, kcat, tq, tk):
    b, _, t, _ = qcat.shape
    qi, kj = _tile_pairs(t, tq, tk)
    grid_spec = pltpu.PrefetchScalarGridSpec(
        num_scalar_prefetch=2, grid=(b, int(qi.shape[0])),
        in_specs=[pl.BlockSpec((1, H_MLA, tq, 256), lambda i, p, qi, kj: (i, 0, qi[p], 0)),
                  pl.BlockSpec((1, tk, 256), lambda i, p, qi, kj: (i, kj[p], 0))],
        out_specs=pl.BlockSpec((1, tq, H_MLA * KV_RANK), lambda i, p, qi, kj: (i, qi[p], 0)),
        scratch_shapes=[pltpu.VMEM((H_MLA * tq, 1), F32), pltpu.VMEM((H_MLA * tq, 1), F32),
                        pltpu.VMEM((H_MLA * tq, KV_RANK), F32)])
    return pl.pallas_call(
        functools.partial(_mla_flash_kernel, tq=tq, tk=tk), grid_spec=grid_spec,
        out_shape=jax.ShapeDtypeStruct((b, t, H_MLA * KV_RANK), F32),
        compiler_params=_params(("parallel", "arbitrary")), name="mla_flash",
    )(qi, kj, qcat, kcat)


def _fox_flash_kernel(qi_ref, kj_ref, q_ref, k_ref, v_ref, o_ref, m_ref, l_ref, acc_ref, *, t):
    p = pl.program_id(1)
    qi = qi_ref[p]
    kj = kj_ref[p]

    @pl.when(kj == 0)
    def _():
        m_ref[...] = jnp.full_like(m_ref, NEG_INF)
        l_ref[...] = jnp.zeros_like(l_ref)
        acc_ref[...] = jnp.zeros_like(acc_ref)

    def heads(masked):
        def body(hd, carry):
            g = hd // 2
            s = _mm_nt(q_ref[0, hd], k_ref[0, g])
            if masked:
                row = lax.broadcasted_iota(jnp.int32, (t, t), 0)
                col = lax.broadcasted_iota(jnp.int32, (t, t), 1)
                s = jnp.where(col <= row, s, NEG_INF)
            m_old = m_ref[hd]
            m_new = jnp.maximum(m_old, jnp.max(s, axis=1, keepdims=True))
            alpha = jnp.exp(m_old - m_new)
            pr = jnp.exp(s - m_new)
            l_ref[hd] = alpha * l_ref[hd] + jnp.sum(pr, axis=1, keepdims=True)
            acc_ref[hd] = alpha * acc_ref[hd] + _mm(pr.astype(BF16), v_ref[0, g])
            m_ref[hd] = m_new
            return carry
        lax.fori_loop(0, H_FOX, body, 0)

    pl.when(kj == qi)(functools.partial(heads, True))
    pl.when(kj != qi)(functools.partial(heads, False))

    @pl.when(kj == qi)
    def _():
        lane = lax.broadcasted_iota(jnp.int32, (1, LANES), 1)
        for g in range(H_FOX // 2):
            oa = acc_ref[2 * g] / l_ref[2 * g]
            ob = acc_ref[2 * g + 1] / l_ref[2 * g + 1]
            o_ref[0, :, g * LANES:(g + 1) * LANES] = jnp.where(lane < D_FOX, oa, ob)


def _fox_flash(qcat, kcat, vb, t_blk):
    b, _, t, _ = qcat.shape
    qi, kj = _tile_pairs(t, t_blk, t_blk)
    grid_spec = pltpu.PrefetchScalarGridSpec(
        num_scalar_prefetch=2, grid=(b, int(qi.shape[0])),
        in_specs=[pl.BlockSpec((1, H_FOX, t_blk, 256), lambda i, p, qi, kj: (i, 0, qi[p], 0)),
                  pl.BlockSpec((1, H_FOX // 2, t_blk, 256), lambda i, p, qi, kj: (i, 0, kj[p], 0)),
                  pl.BlockSpec((1, H_FOX // 2, t_blk, LANES), lambda i, p, qi, kj: (i, 0, kj[p], 0))],
        out_specs=pl.BlockSpec((1, t_blk, FOX_W), lambda i, p, qi, kj: (i, qi[p], 0)),
        scratch_shapes=[pltpu.VMEM((H_FOX, t_blk, 1), F32), pltpu.VMEM((H_FOX, t_blk, 1), F32),
                        pltpu.VMEM((H_FOX, t_blk, LANES), F32)])
    return pl.pallas_call(
        functools.partial(_fox_flash_kernel, t=t_blk), grid_spec=grid_spec,
        out_shape=jax.ShapeDtypeStruct((b, t, FOX_W), F32),
        compiler_params=_params(("parallel", "arbitrary")), name="fox_flash",
    )(qi, kj, qcat, kcat, vb)


def _softmax_step(s, m_ref, l_ref):
    m_old = m_ref[...]
    m_new = jnp.maximum(m_old, jnp.max(s, axis=1, keepdims=True))
    alpha = jnp.exp(m_old - m_new)
    pr = jnp.exp(s - m_new)
    l_ref[...] = alpha * l_ref[...] + jnp.sum(pr, axis=1, keepdims=True)
    m_ref[...] = m_new
    return pr, alpha


def _mla_decode_kernel(pt_ref, *refs, pps, nchunk, n_new):
    ckv_pages = refs[:pps]
    krt_pages = refs[pps:2 * pps]
    qa_ref, qr_ref, ckn_ref, krn_ref, o_ref, m_ref, l_ref, acc_ref = refs[2 * pps:]
    c = pl.program_id(1)

    @pl.when(c == 0)
    def _():
        m_ref[...] = jnp.full_like(m_ref, NEG_INF)
        l_ref[...] = jnp.zeros_like(l_ref)
        acc_ref[...] = jnp.zeros_like(acc_ref)

    qa = qa_ref[0]
    qr = qr_ref[0]

    def update(ckv, krt, mask):
        s = _mm_nt(qa, ckv) + _mm(qr, krt)
        if mask is not None:
            s = jnp.where(mask, s, NEG_INF)
        pr, alpha = _softmax_step(s, m_ref, l_ref)
        acc_ref[...] = alpha * acc_ref[...] + _mm(pr.astype(BF16), ckv)

    ckv = jnp.concatenate([r[0, 0].astype(BF16) for r in ckv_pages], axis=0)
    krt = jnp.concatenate([r[0, 0].astype(BF16) for r in krt_pages], axis=1)
    update(ckv, krt, None)

    @pl.when(c == nchunk - 1)
    def _():
        key = lax.broadcasted_iota(jnp.int32, (qa.shape[0], PAGE), 1)
        qry = lax.broadcasted_iota(jnp.int32, (qa.shape[0], PAGE), 0) % n_new
        update(ckn_ref[0].astype(BF16), krn_ref[0].astype(BF16), key <= qry)
        o_ref[0] = acc_ref[...] / l_ref[...]


def _mla_decode(layer, page_table, pool_ckv, pool_krt, qa, qr, ckn, krnt, pps):
    bs, npages = page_table.shape
    nchunk = npages // pps
    n_new = 8
    rows = H_MLA * n_new
    pt = page_table.reshape(-1)

    def page_spec(j, shape):
        return pl.BlockSpec((1, 1) + shape, lambda b, c, pt: (layer, pt[b * npages + c * pps + j], 0, 0))

    per_seq = lambda shape: pl.BlockSpec((1,) + shape, lambda b, c, pt: (b,) + (0,) * len(shape))
    in_specs = ([page_spec(j, (PAGE, KV_RANK)) for j in range(pps)]
                + [page_spec(j, (ROPE, PAGE)) for j in range(pps)]
                + [per_seq((rows, KV_RANK)), per_seq((rows, ROPE)), per_seq((PAGE, KV_RANK)),
                   per_seq((ROPE, PAGE))])
    grid_spec = pltpu.PrefetchScalarGridSpec(
        num_scalar_prefetch=1, grid=(bs, nchunk), in_specs=in_specs,
        out_specs=per_seq((rows, KV_RANK)),
        scratch_shapes=[pltpu.VMEM((rows, 1), F32), pltpu.VMEM((rows, 1), F32), pltpu.VMEM((rows, KV_RANK), F32)])
    return pl.pallas_call(
        functools.partial(_mla_decode_kernel, pps=pps, nchunk=nchunk, n_new=n_new), grid_spec=grid_spec,
        out_shape=jax.ShapeDtypeStruct((bs, rows, KV_RANK), F32),
        compiler_params=_params(("parallel", "arbitrary")), name="mla_decode",
    )(pt, *([pool_ckv] * pps), *([pool_krt] * pps), qa, qr, ckn, krnt)


def _fox_decode_kernel(pt_ref, *refs, pps, nchunk, n_new):
    kt_pages = refs[:pps]
    vt_pages = refs[pps:2 * pps]
    lf_pages = refs[2 * pps:3 * pps]
    (qbd_ref, triu_ref, knt_ref, vnt_ref, lfn_ref, o_ref, m_ref, l_ref, acc_ref, carry_ref) = refs[3 * pps:]
    c = pl.program_id(1)
    rows = n_new * H_FOX

    @pl.when(c == 0)
    def _():
        m_ref[...] = jnp.full_like(m_ref, NEG_INF)
        l_ref[...] = jnp.zeros_like(l_ref)
        acc_ref[...] = jnp.zeros_like(acc_ref)
        carry_ref[...] = jnp.zeros_like(carry_ref)

    qbd = qbd_ref[0]
    triu = triu_ref[...]

    def col_to_row(x):
        return jnp.transpose(jnp.broadcast_to(x, (rows, LANES)))[0:1, :]

    def update(kt, vt, lf_rows, mask):
        n = lf_rows.shape[0] // H_FOX
        cw = jnp.zeros(lf_rows.shape, F32)
        for part in _split3(lf_rows):
            cw = cw + _mm(part, triu)
        carry = carry_ref[...]
        bias = []
        for j in range(n):
            cj = cw[j * H_FOX:(j + 1) * H_FOX] + carry
            bias.append(cj)
            carry = cj[:, PAGE - 1:PAGE]
        carry_ref[...] = carry
        b = jnp.concatenate(bias, axis=1) if n > 1 else bias[0]
        s = _mm(qbd, kt)
        s = (s.reshape(n_new, H_FOX, s.shape[1]) - b[None]).reshape(rows, s.shape[1])
        if mask is not None:
            s = jnp.where(mask, s, NEG_INF)
        pr, alpha = _softmax_step(s, m_ref, l_ref)
        acc_ref[...] = col_to_row(alpha) * acc_ref[...] + _mm_nt(vt, pr.astype(BF16))

    kt = jnp.concatenate([r[0, 0].astype(BF16) for r in kt_pages], axis=1)
    vt = jnp.concatenate([r[0, 0].astype(BF16) for r in vt_pages], axis=1)
    lf = jnp.concatenate([r[0, 0] for r in lf_pages], axis=0)
    update(kt, vt, lf, None)

    @pl.when(c == nchunk - 1)
    def _():
        key = lax.broadcasted_iota(jnp.int32, (rows, PAGE), 1)
        qry = lax.broadcasted_iota(jnp.int32, (rows, PAGE), 0) // H_FOX
        update(knt_ref[0].astype(BF16), vnt_ref[0].astype(BF16), lfn_ref[0], key <= qry)
        lane = lax.broadcasted_iota(jnp.int32, (1, rows), 1) % H_FOX
        o = jnp.zeros((D_FOX, rows), F32)
        for hd in range(H_FOX):
            o = o + jnp.where(lane == hd, acc_ref[hd * D_FOX:(hd + 1) * D_FOX, :], 0.0)
        o_ref[0] = o / col_to_row(l_ref[...])


def _fox_decode(layer, page_table, pool_kt, pool_vt, pool_lft, qbd, knt, vnt, lfnt, consts, pps):
    bs, npages = page_table.shape
    nchunk = npages // pps
    n_new = 8
    rows = n_new * H_FOX
    pt = page_table.reshape(-1)

    def page_spec(j, shape):
        return pl.BlockSpec((1, 1) + shape, lambda b, c, pt: (layer, pt[b * npages + c * pps + j], 0, 0))

    per_seq = lambda shape: pl.BlockSpec((1,) + shape, lambda b, c, pt: (b,) + (0,) * len(shape))
    const = lambda shape: pl.BlockSpec(shape, lambda b, c, pt: (0,) * len(shape))
    in_specs = ([page_spec(j, (FOX_W, PAGE)) for j in range(pps)]
                + [page_spec(j, (FOX_W, PAGE)) for j in range(pps)]
                + [page_spec(j, (H_FOX, PAGE)) for j in range(pps)]
                + [per_seq((rows, FOX_W)), const((PAGE, PAGE)),
                   per_seq((FOX_W, PAGE)), per_seq((FOX_W, PAGE)), per_seq((H_FOX, PAGE))])
    grid_spec = pltpu.PrefetchScalarGridSpec(
        num_scalar_prefetch=1, grid=(bs, nchunk), in_specs=in_specs,
        out_specs=per_seq((D_FOX, rows)),
        scratch_shapes=[pltpu.VMEM((rows, 1), F32), pltpu.VMEM((rows, 1), F32),
                        pltpu.VMEM((FOX_W, rows), F32), pltpu.VMEM((H_FOX, 1), F32)])
    return pl.pallas_call(
        functools.partial(_fox_decode_kernel, pps=pps, nchunk=nchunk, n_new=n_new), grid_spec=grid_spec,
        out_shape=jax.ShapeDtypeStruct((bs, D_FOX, rows), F32),
        compiler_params=_params(("parallel", "arbitrary")), name="fox_decode",
    )(pt, *([pool_kt] * pps), *([pool_vt] * pps), *([pool_lft] * pps), qbd, consts["triu_page"],
      knt, vnt, lfnt)


def _outproj_kernel(x_ref, ol_ref, of_ref, wuv_ref, gm_ref, gf_ref, wo_ref, g1_ref, b1_ref, wmq_ref,
                    x1_ref, mq_ref):
    o_mla = _mm(ol_ref[0].astype(BF16), wuv_ref[...])
    nm = _rms(o_mla, gm_ref[...]).astype(BF16)
    nf = _rms(of_ref[0], gf_ref[...]).astype(BF16)
    mix = _mm(jnp.concatenate([nm, nf], axis=1), wo_ref[...])
    x1 = _ln(ALPHA * x_ref[0] + mix, g1_ref[...], b1_ref[...])
    x1_ref[0] = x1
    mq_ref[0] = _mm(x1.astype(BF16), wmq_ref[...]) * MEM_SCALE


def _outproj(x, o_lat, o_fox, lw, tm):
    b, t, d = x.shape
    tok = lambda w: pl.BlockSpec((1, tm, w), lambda i, j: (i, j, 0))
    in_specs = [tok(d), tok(H_MLA * KV_RANK), tok(FOX_W),
                _const_spec(lw["w_uv"].shape), _const_spec((1, FOX_W)), _const_spec((1, FOX_W)),
                _const_spec(lw["w_o"].shape), _const_spec((1, d)), _const_spec((1, d)),
                _const_spec(lw["w_mq"].shape)]
    return pl.pallas_call(
        _outproj_kernel, grid=(b, t // tm), in_specs=in_specs, out_specs=(tok(d), tok(d)),
        out_shape=(jax.ShapeDtypeStruct((b, t, d), F32), jax.ShapeDtypeStruct((b, t, d), F32)),
        compiler_params=_params(("parallel", "parallel")), name="outproj",
    )(x, o_lat, o_fox, lw["w_uv"], lw["g_mla_out"], lw["g_fox_out"], lw["w_o"], lw["ln1_g"], lw["ln1_b"],
      lw["w_mq"])


def _memkv_kernel(m_ref, wk_ref, wv_ref, k_ref, v_ref):
    mb = m_ref[...].astype(BF16)
    k_ref[...] = _mm(mb, wk_ref[...])
    v_ref[...] = _mm(mb, wv_ref[...])


def _memkv(mem, wk, wv):
    n, d = mem.shape
    return pl.pallas_call(
        _memkv_kernel, grid=(1,),
        in_specs=[_const_spec((n, d)), _const_spec(wk.shape), _const_spec(wv.shape)],
        out_specs=(_const_spec((n, d)), _const_spec((n, d))),
        out_shape=(jax.ShapeDtypeStruct((n, d), F32), jax.ShapeDtypeStruct((n, d), F32)),
        compiler_params=_params(("arbitrary",)), name="memkv",
    )(mem, wk, wv)


def _cross_kernel(mq_ref, mk_ref, mv_ref, o_ref, *, split_heads):
    mq = mq_ref[0].astype(BF16)
    outs = []
    for hd in range(H_MEM):
        if split_heads:
            mk = mk_ref[0, 0, :, hd, :]
            mv = mv_ref[0, 0, :, hd, :]
        else:
            mk = mk_ref[0][:, hd * D_MEM:(hd + 1) * D_MEM]
            mv = mv_ref[0][:, hd * D_MEM:(hd + 1) * D_MEM]
        s = _mm_nt(mq[:, hd * D_MEM:(hd + 1) * D_MEM], mk.astype(BF16))
        pr = jnp.exp(s - jnp.max(s, axis=1, keepdims=True))
        pr = pr / jnp.sum(pr, axis=1, keepdims=True)
        outs.append(_mm(pr.astype(BF16), mv.astype(BF16)))
    o_ref[0] = jnp.concatenate(outs, axis=1)


def _cross(mq, mk, mv, tm, layer=None):
    b, t, d = mq.shape
    tok = pl.BlockSpec((1, tm, d), lambda i, j: (i, j, 0))
    if layer is None:
        mem = pl.BlockSpec((1, MEM_TOK, d), lambda i, j: (i, 0, 0))
    else:
        mem = pl.BlockSpec((1, 1, MEM_TOK, H_MEM, D_MEM), lambda i, j: (layer, i, 0, 0, 0))
    return pl.pallas_call(
        functools.partial(_cross_kernel, split_heads=layer is not None), grid=(b, t // tm),
        in_specs=[tok, mem, mem], out_specs=tok, out_shape=jax.ShapeDtypeStruct((b, t, d), F32),
        compiler_params=_params(("parallel", "parallel")), name="cross",
    )(mq, mk, mv)


def _ffn_kernel(*refs, seq_mode, n_chunks):
    if seq_mode:
        (x_ref, mo_ref, wmo_ref, g2_ref, b2_ref, wup_ref, cw_ref, cb_ref, wdn_ref, g3_ref, b3_ref,
         y_ref, u_ref, carry_ref) = refs
    else:
        (x_ref, mo_ref, pre_ref, wmo_ref, g2_ref, b2_ref, wup_ref, cw_ref, cb_ref, wdn_ref, g3_ref, b3_ref,
         y_ref, u_ref) = refs
    tm = x_ref.shape[1]
    dff = wdn_ref.shape[0]
    fc = dff // n_chunks

    if seq_mode:
        @pl.when(pl.program_id(1) == 0)
        def _():
            carry_ref[...] = jnp.zeros_like(carry_ref)

    x2 = _ln(ALPHA * x_ref[0] + _mm(mo_ref[0].astype(BF16), wmo_ref[...]), g2_ref[...], b2_ref[...])
    xb = x2.astype(BF16)
    row = lax.broadcasted_iota(jnp.int32, (tm, 1), 0)

    def conv(lo):
        u = _mm(xb, wup_ref[:, lo:lo + fc])
        if seq_mode:
            ext = jnp.concatenate([carry_ref[:, lo:lo + fc], u], axis=0)
            u1 = pltpu.roll(ext, 1, 0)[SUBLANES:]
            u2 = pltpu.roll(ext, 2, 0)[SUBLANES:]
            carry_ref[:, lo:lo + fc] = u[tm - SUBLANES:]
            u_ref[0, 0, :, lo:lo + fc] = u[tm - SUBLANES:]
        else:
            e = pre_ref[0, :, lo:lo + fc]
            u1 = jnp.where(row % SUBLANES < 1, pltpu.roll(e, tm - 1, 0), pltpu.roll(u, 1, 0))
            u2 = jnp.where(row % SUBLANES < 2, e, pltpu.roll(u, 2, 0))
            u_ref[0, :, lo:lo + fc] = u
        cw = cw_ref[:, lo:lo + fc]
        return cb_ref[:, lo:lo + fc] + (cw[0:1] * u2 + cw[1:2] * u1 + cw[2:3] * u)

    ffn = jnp.zeros((tm, x_ref.shape[2]), F32)
    for ci in range(n_chunks):
        val = conv(ci * fc)
        gate = conv(dff + ci * fc)
        hcat = (gate * jax.nn.sigmoid(gate) * val).astype(BF16)
        ffn = ffn + _mm(hcat, wdn_ref[ci * fc:(ci + 1) * fc, :])
    y_ref[0] = _ln(ALPHA * x2 + ffn, g3_ref[...], b3_ref[...])


def _ffn(x1, mo, lw, tm, pre=None):
    b, t, d = x1.shape
    f2 = lw["w_up"].shape[1]
    seq_mode = pre is None
    tok = lambda w: pl.BlockSpec((1, tm, w), lambda i, j: (i, j, 0))
    consts = [_const_spec(lw["w_mo"].shape), _const_spec((1, d)), _const_spec((1, d)),
              _const_spec(lw["w_up"].shape), _const_spec((3, f2)), _const_spec((1, f2)),
              _const_spec(lw["w_down"].shape), _const_spec((1, d)), _const_spec((1, d))]
    cargs = (lw["w_mo"], lw["ln2_g"], lw["ln2_b"], lw["w_up"], lw["conv_w"], lw["conv_b"], lw["w_down"],
             lw["ln3_g"], lw["ln3_b"])
    if seq_mode:
        in_specs = [tok(d), tok(d)] + consts
        args = (x1, mo) + cargs
        u_shape = jax.ShapeDtypeStruct((b, t // tm, SUBLANES, f2), F32)
        u_spec = pl.BlockSpec((1, 1, SUBLANES, f2), lambda i, j: (i, j, 0, 0))
        scratch = [pltpu.VMEM((SUBLANES, f2), F32)]
    else:
        in_specs = [tok(d), tok(d), tok(f2)] + consts
        args = (x1, mo, pre) + cargs
        u_shape = jax.ShapeDtypeStruct((b, t, f2), F32)
        u_spec = tok(f2)
        scratch = []
    return pl.pallas_call(
        functools.partial(_ffn_kernel, seq_mode=seq_mode, n_chunks=2), grid=(b, t // tm),
        in_specs=in_specs, out_specs=(tok(d), u_spec),
        out_shape=(jax.ShapeDtypeStruct((b, t, d), F32), u_shape), scratch_shapes=scratch,
        compiler_params=_params(("parallel", "arbitrary")), name="ffn",
    )(*args)


def _layer_weights(l, w_in, b_forget, g_q_lat, g_kv_lat, w_uq, w_uk, w_uv, g_mla_out, g_fox_out, w_o, ln1_g,
                   ln1_b, w_mq, w_mk, w_mv, w_mo, ln2_g, ln2_b, w_up, conv_w, conv_b, w_down, ln3_g, ln3_b):
    wi = w_in[l]
    d = wi.shape[0]
    o_kr = Q_RANK + KV_RANK
    o_fq = o_kr + ROPE
    half = ROPE // 2
    kr = wi[:, o_kr:o_fq]
    kr_sw = jnp.concatenate([kr[:, half:], kr[:, :half]], axis=1)
    n_rep = LANES // ROPE
    w_aug = jnp.concatenate([
        wi[:, :o_kr], jnp.tile(kr, (1, n_rep)), jnp.tile(kr_sw, (1, n_rep)), wi[:, o_fq:o_fq + 3 * FOX_W],
        jnp.pad(wi[:, o_fq + 3 * FOX_W:], ((0, 0), (0, LANES - H_FOX)))], axis=1).astype(BF16)
    uq = w_uq[l]
    uq_rope = uq[:, :, NOPE:]
    uq_sw = jnp.concatenate([uq_rope[:, :, half:], uq_rope[:, :, :half]], axis=2)
    w_uq_aug = jnp.concatenate([uq[:, :, :NOPE].reshape(Q_RANK, -1), uq_rope.reshape(Q_RANK, -1),
                                uq_sw.reshape(Q_RANK, -1)], axis=1).astype(BF16)
    uk_t = jnp.transpose(w_uk[l], (1, 2, 0))
    parity = jax.nn.one_hot(jnp.arange(H_MLA) % 2, 2, dtype=F32)
    w_uk_pad = (parity[:, :, None, None] * uk_t[:, None]).reshape(H_MLA, 2 * NOPE, KV_RANK).astype(BF16)
    w_uv_bd = jnp.einsum("rhv,hg->hrgv", w_uv[l], jnp.eye(H_MLA, dtype=F32)).reshape(
        H_MLA * KV_RANK, H_MLA * V_DIM).astype(BF16)
    lane = jnp.arange(LANES)
    inv = ROPE_BASE ** (-jnp.arange(half, dtype=F32) / half)
    row = lambda v: v.reshape(1, -1).astype(F32)
    return dict(
        w_in=w_aug, b_forget=row(jnp.pad(b_forget[l], (0, LANES - H_FOX))), g_q=row(g_q_lat[l]),
        g_kv=row(g_kv_lat[l]), w_uq=w_uq_aug, w_uk=w_uk_pad, inv_freq=row(inv[lane % half]),
        rope_sign=row(jnp.where(lane % ROPE < half, -1.0, 1.0)),
        w_uv=w_uv_bd, g_mla_out=row(g_mla_out[l]), g_fox_out=row(g_fox_out[l]), w_o=w_o[l].astype(BF16),
        ln1_g=row(ln1_g[l]), ln1_b=row(ln1_b[l]), w_mq=w_mq[l].reshape(d, d).astype(BF16),
        w_mk=w_mk[l].reshape(d, d).astype(BF16), w_mv=w_mv[l].reshape(d, d).astype(BF16),
        w_mo=w_mo[l].reshape(d, d).astype(BF16), ln2_g=row(ln2_g[l]), ln2_b=row(ln2_b[l]),
        w_up=w_up[l].astype(BF16), conv_w=conv_w[l].astype(F32), conv_b=row(conv_b[l]),
        w_down=w_down[l].astype(BF16), ln3_g=row(ln3_g[l]), ln3_b=row(ln3_b[l]))


def _constants(tm):
    tri = (np.arange(tm)[:, None] >= np.arange(tm)[None, :]).astype(np.float32)
    place_q = np.zeros((3, H_FOX, LANES), np.float32)
    place_k = np.zeros((3, H_FOX, LANES), np.float32)
    ones_q = np.zeros((1, LANES), np.float32)
    ones_k = np.zeros((1, LANES), np.float32)
    for hd in range(H_FOX):
        for j in range(3):
            place_q[j, hd, hd * 8 + j] = 1.0
            place_k[j, hd, hd * 8 + 3 + j] = -1.0
            ones_q[0, hd * 8 + 3 + j] = 1.0
            ones_k[0, hd * 8 + j] = 1.0
    head_mask = (np.arange(FOX_W)[None, :] // D_FOX == np.arange(H_FOX)[:, None]).astype(np.float32)
    return dict(tri=jnp.asarray(tri, BF16), place_q=jnp.asarray(place_q, BF16),
                place_k=jnp.asarray(place_k, BF16), ones_q=jnp.asarray(ones_q), ones_k=jnp.asarray(ones_k),
                triu_page=jnp.asarray(tri[:PAGE, :PAGE].T, BF16), head_mask=jnp.asarray(head_mask, BF16))


def _tile(n, pref):
    return pref if n % pref == 0 else n


def _prompt_layer(x, pos, mem, lw, consts):
    b, t, d = x.shape
    tm = _tile(t, 512)
    qcat, kcat, ckv, kr, fq, fk, fv, lf = _inproj(x, pos, lw, tm)
    fqc, fkc, fvb = _foxprep(fq, fk, fv, lf, consts, tm)
    o_lat = _mla_flash(qcat, kcat, _tile(t, 256), tm)
    o_fox = _fox_flash(fqc, fkc, fvb, tm)
    x1, mq = _outproj(x, o_lat, o_fox, lw, tm)
    mk, mv = _memkv(mem.reshape(b * MEM_TOK, d), lw["w_mk"], lw["w_mv"])
    mk = mk.reshape(b, MEM_TOK, d)
    mv = mv.reshape(b, MEM_TOK, d)
    mo = _cross(mq, mk, mv, tm)
    y, u_tail = _ffn(x1, mo, lw, tm)
    state = (ckv, kr, fk.reshape(b, t, H_FOX, D_FOX), fv.reshape(b, t, H_FOX, D_FOX), lf,
             mk.reshape(b, MEM_TOK, H_MEM, D_MEM), mv.reshape(b, MEM_TOK, H_MEM, D_MEM),
             u_tail[:, -1, SUBLANES - 2:, :])
    return y, state


def _sample_layer(l, x, pos, caches, page_table, lw, consts):
    bs, ts, d = x.shape
    n = bs * ts
    tm = _tile(n, 512)
    c_ckv, c_kr, c_fk, c_fv, c_lf, c_mk, c_mv, c_conv = caches
    xf = x.reshape(1, n, d)
    qcat, kcat, ckv, kr, fq, fk, fv, lf = _inproj(xf, pos, lw, tm)
    q4 = qcat[0].reshape(H_MLA, bs, ts, 256)
    qa = jnp.transpose(q4[..., :KV_RANK], (1, 0, 2, 3)).reshape(bs, H_MLA * ts, KV_RANK)
    qr = jnp.stack([q4[hd, :, :, KV_RANK + (hd % 4) * ROPE:KV_RANK + (hd % 4 + 1) * ROPE]
                    for hd in range(H_MLA)], axis=1).reshape(bs, H_MLA * ts, ROPE)
    keys_last = lambda a: jnp.pad(jnp.swapaxes(a.reshape(bs, ts, -1), 1, 2), ((0, 0), (0, 0), (0, PAGE - ts)))
    npages = page_table.shape[1]
    ckn = jnp.pad(ckv.reshape(bs, ts, KV_RANK), ((0, 0), (0, PAGE - ts), (0, 0)))
    o_lat = _mla_decode(l, page_table, c_ckv, c_kr, qa, qr, ckn, keys_last(kr), _tile(npages, 16))
    o_lat = jnp.transpose(o_lat.reshape(bs, H_MLA, ts, KV_RANK), (0, 2, 1, 3)).reshape(1, n, H_MLA * KV_RANK)
    qbd = (fq.reshape(bs, ts, 1, FOX_W) * consts["head_mask"][None, None]).reshape(bs, ts * H_FOX, FOX_W)
    o_fox_t = _fox_decode(l, page_table, c_fk, c_fv, c_lf, qbd, keys_last(fk), keys_last(fv), keys_last(lf),
                          consts, _tile(npages, 16))
    o_fox = jnp.transpose(o_fox_t.reshape(bs, D_FOX, ts, H_FOX), (0, 2, 3, 1)).reshape(1, n, FOX_W)
    fk4 = fk.reshape(bs, ts, H_FOX, D_FOX)
    fv4 = fv.reshape(bs, ts, H_FOX, D_FOX)
    lf3 = lf.reshape(bs, ts, H_FOX)
    x1, mq = _outproj(xf, o_lat, o_fox, lw, tm)
    mo = _cross(mq.reshape(bs, ts, d), c_mk, c_mv, ts, layer=l).reshape(1, n, d)
    pre = jnp.pad(c_conv[l], ((0, 0), (0, ts - c_conv.shape[2]), (0, 0))).reshape(1, n, -1)
    y, u = _ffn(x1, mo, lw, _tile(n, 256), pre=pre)
    state = (ckv.reshape(bs, ts, KV_RANK), kr.reshape(bs, ts, ROPE), fk4, fv4, lf3,
             u.reshape(bs, ts, -1)[:, ts - 2:, :])
    return y.reshape(bs, ts, d), state


def kernel(x_prompt, x_sample, mem_prompt, cache_mla_ckv, cache_mla_krope, cache_fox_k, cache_fox_v, cache_fox_logf, cache_mem_k, cache_mem_v, state_conv, page_table, w_in, b_forget, g_q_lat, g_kv_lat, w_uq, w_uk, w_uv, g_mla_out, g_fox_out, w_o, ln1_g, ln1_b, w_mq, w_mk, w_mv, w_mo, ln2_g, ln2_b, w_up, conv_w, conv_b, w_down, ln3_g, ln3_b):
    bp, tp, _ = x_prompt.shape
    bs, ts, _ = x_sample.shape
    n_past = page_table.shape[1] * PAGE
    pos_p = jnp.broadcast_to(jnp.arange(tp, dtype=F32)[None, :, None], (bp, tp, 1))
    pos_s = (n_past + jnp.arange(bs * ts) % ts).astype(F32).reshape(1, bs * ts, 1)
    consts = _constants(_tile(tp, 512))
    n_pool = cache_fox_k.shape[1]
    kv_view = lambda a: jnp.transpose(a, (0, 1, 3, 4, 2)).reshape(DEPTH, n_pool, FOX_W, PAGE)
    caches = (cache_mla_ckv, jnp.swapaxes(cache_mla_krope, 2, 3), kv_view(cache_fox_k), kv_view(cache_fox_v),
              jnp.swapaxes(cache_fox_logf, 2, 3), cache_mem_k, cache_mem_v, state_conv)
    weights = (w_in, b_forget, g_q_lat, g_kv_lat, w_uq, w_uk, w_uv, g_mla_out, g_fox_out, w_o, ln1_g, ln1_b,
               w_mq, w_mk, w_mv, w_mo, ln2_g, ln2_b, w_up, conv_w, conv_b, w_down, ln3_g, ln3_b)
    xp, xs = x_prompt, x_sample
    p_states, s_states = [], []
    for l in range(DEPTH):
        lw = _layer_weights(l, *weights)
        xp, st = _prompt_layer(xp, pos_p, mem_prompt, lw, consts)
        p_states.append(st)
        xs, st = _sample_layer(l, xs, pos_s, caches, page_table, lw, consts)
        s_states.append(st)
    stack = lambda states, i: jnp.stack([st[i] for st in states])
    return ((xp, xs) + tuple(stack(p_states, i) for i in range(8)) + tuple(stack(s_states, i) for i in range(6)))
```

```python
import functools

import numpy as np
import jax
import jax.numpy as jnp
from jax import lax
from jax.experimental import pallas as pl
from jax.experimental.pallas import tpu as pltpu

F32 = jnp.float32
BF16 = jnp.bfloat16

DEPTH = 2
PAGE = 128
H_MLA = 8
NOPE = 64
ROPE = 32
V_DIM = 64
Q_RANK = 256
KV_RANK = 128
MLA_SCALE = (NOPE + ROPE) ** -0.5
ROPE_BASE = 10000.0
H_FOX = 8
D_FOX = 64
FOX_W = H_FOX * D_FOX
FOX_SCALE = D_FOX ** -0.5
MEM_TOK = 256
H_MEM = 4
D_MEM = 256
MEM_SCALE = D_MEM ** -0.5
LN_EPS = 1e-5
RMS_EPS = 1e-6
ALPHA = (2 * DEPTH) ** 0.25

LANES = 128
SUBLANES = 8
VMEM_LIMIT = 56 * 1024 * 1024

C_Q, C_KV, C_KR, C_KS, C_FQ, C_FK, C_FV, C_FL, C_END = 0, 256, 384, 512, 640, 1152, 1664, 2176, 2304

NEG_INF = float("-inf")


def _mm(a, b):
    return jnp.dot(a, b, preferred_element_type=F32)


def _mm_nt(a, b):
    return lax.dot_general(a, b, (((1,), (1,)), ((), ())), preferred_element_type=F32)


def _mm_tn(a, b):
    return lax.dot_general(a, b, (((0,), (0,)), ((), ())), preferred_element_type=F32)


def _rms(x, g):
    return x * lax.rsqrt(jnp.mean(x * x, axis=-1, keepdims=True) + RMS_EPS) * g


def _ln(x, g, b):
    mu = jnp.mean(x, axis=-1, keepdims=True)
    xc = x - mu
    var = jnp.mean(xc * xc, axis=-1, keepdims=True)
    return xc * lax.rsqrt(var + LN_EPS) * g + b


def _split3(x):
    hi = x.astype(BF16)
    r = x - hi.astype(F32)
    mid = r.astype(BF16)
    lo = (r - mid.astype(F32)).astype(BF16)
    return hi, mid, lo


def _const_spec(shape):
    nd = len(shape)
    return pl.BlockSpec(shape, lambda *_: (0,) * nd)


def _params(sem):
    return pltpu.CompilerParams(dimension_semantics=sem, vmem_limit_bytes=VMEM_LIMIT)


def _inproj_kernel(x_ref, pos_ref, w_ref, bf_ref, gq_ref, gkv_ref, wuq_ref, wuk_ref, inv_ref, sgn_ref,
                   qcat_ref, kcat_ref, vcat_ref, ckv_ref, kr_ref, fq_ref, fk_ref, fv_ref, lf_ref):
    xb = x_ref[0].astype(BF16)
    h = _mm(xb, w_ref[...])
    ang = pos_ref[0] * inv_ref[...]
    cos = jnp.cos(ang)
    sin = jnp.sin(ang) * sgn_ref[...]
    qn = _rms(h[:, C_Q:C_KV], gq_ref[...]).astype(BF16)
    qu = _mm(qn, wuq_ref[...])
    ckv = _rms(h[:, C_KV:C_KR], gkv_ref[...])
    kr = h[:, C_KR:C_KS] * cos + h[:, C_KS:C_FQ] * sin
    ckv_ref[0] = ckv
    kr_ref[0] = kr[:, :ROPE]
    kcat_ref[0] = jnp.concatenate([ckv.astype(BF16), kr.astype(BF16)], axis=1)
    vcat_ref[0] = jnp.concatenate([ckv.astype(BF16), jnp.ones(ckv.shape, BF16)], axis=1)
    lane =lax.broadcasted_iota(jnp.int32, (1, LANES), 1)
    slabs = []
    for g in range(2):
        lo = 512 + g * LANES
        slabs.append(qu[:, lo:lo + LANES] * cos + qu[:, lo + 256:lo + 256 + LANES] * sin)
    for hd in range(H_MLA):
        g2 = hd // 2
        qabs = _mm(qu[:, g2 * LANES:(g2 + 1) * LANES].astype(BF16), wuk_ref[hd])
        slab = jnp.where(lane // ROPE == hd % 4, slabs[hd // 4], 0.0)
        qcat_ref[0, hd] = jnp.concatenate(
            [(qabs * MLA_SCALE).astype(BF16), (slab * MLA_SCALE).astype(BF16)], axis=1)
    fq_ref[0] = (h[:, C_FQ:C_FK] * FOX_SCALE).astype(BF16)
    fk_ref[0] = h[:, C_FK:C_FV]
    fv_ref[0] = h[:, C_FV:C_FL]
    z = h[:, C_FL:C_END] + bf_ref[...]
    logf = jnp.minimum(z, 0.0) - jnp.log1p(jnp.exp(-jnp.abs(z)))
    lf_ref[0] = logf[:, :H_FOX]


def _inproj(x, pos, lw, tm):
    b, t, d = x.shape
    grid = (b, t // tm)
    tok = lambda w: pl.BlockSpec((1, tm, w), lambda i, j: (i, j, 0))
    out_shape = (
        jax.ShapeDtypeStruct((b, H_MLA, t, 256), BF16),
        jax.ShapeDtypeStruct((b, t, 256), BF16),
        jax.ShapeDtypeStruct((b, t, 256), BF16),
        jax.ShapeDtypeStruct((b, t, KV_RANK), F32),
        jax.ShapeDtypeStruct((b, t, ROPE), F32),
        jax.ShapeDtypeStruct((b, t, FOX_W), BF16),
        jax.ShapeDtypeStruct((b, t, FOX_W), F32),
        jax.ShapeDtypeStruct((b, t, FOX_W), F32),
        jax.ShapeDtypeStruct((b, t, H_FOX), F32),
    )
    out_specs = (
        pl.BlockSpec((1, H_MLA, tm, 256), lambda i, j: (i, 0, j, 0)),
        tok(256), tok(256), tok(KV_RANK), tok(ROPE), tok(FOX_W), tok(FOX_W), tok(FOX_W), tok(H_FOX),
    )
    in_specs = [
        tok(d), tok(1),
        _const_spec(lw["w_in"].shape), _const_spec((1, LANES)), _const_spec((1, Q_RANK)),
        _const_spec((1, KV_RANK)), _const_spec(lw["w_uq"].shape), _const_spec(lw["w_uk"].shape),
        _const_spec((1, LANES)), _const_spec((1, LANES)),
    ]
    return pl.pallas_call(
        _inproj_kernel, grid=grid, in_specs=in_specs, out_specs=out_specs, out_shape=out_shape,
        compiler_params=_params(("parallel", "parallel")), name="inproj",
    )(x, pos, lw["w_in"], lw["b_forget"], lw["g_q"], lw["g_kv"], lw["w_uq"], lw["w_uk"],
      lw["inv_freq"], lw["rope_sign"])


def _foxprep_kernel(fq_ref, fk_ref, fv_ref, lf_ref, tri_ref, pq_ref, pk_ref, oq_ref, ok_ref,
                    qcat_ref, kcat_ref, vb_ref, carry_ref):
    tm = lf_ref.shape[1]

    @pl.when(pl.program_id(1) == 0)
    def _():
        carry_ref[...] = jnp.zeros_like(carry_ref)

    tri = tri_ref[...]
    c = carry_ref[...]
    for part in _split3(lf_ref[0]):
        c = c + _mm(tri, part)
    carry_ref[...] = c[tm - 1:tm, :]
    parts = _split3(c)
    augq = oq_ref[...]
    augk = ok_ref[...]
    for j in range(3):
        augq = augq + _mm(parts[j], pq_ref[j])
        augk = augk + _mm(parts[j], pk_ref[j])
    augk = augk.astype(BF16)
    lane = lax.broadcasted_iota(jnp.int32, (1, LANES), 1)
    fq = fq_ref[0]
    fkb = fk_ref[0].astype(BF16)
    fvb = fv_ref[0].astype(BF16)
    for g in range(H_FOX // 2):
        kcat_ref[0, g] = jnp.concatenate([fkb[:, g * LANES:(g + 1) * LANES], augk], axis=1)
        vb_ref[0, g] = jnp.concatenate([fvb[:, g * LANES:(g + 1) * LANES], jnp.ones((tm, LANES), BF16)], axis=1)
    for hd in range(H_FOX):
        g = hd // 2
        qs = jnp.where(lane // D_FOX == hd % 2, fq[:, g * LANES:(g + 1) * LANES].astype(F32), 0.0)
        aq = jnp.where(lane // 8 == hd, augq, 0.0)
        qcat_ref[0, hd] = jnp.concatenate([qs.astype(BF16), aq.astype(BF16)], axis=1)


def _foxprep(fq, fk, fv, lf, consts, tm):
    b, t, _ = fq.shape
    tok = lambda w: pl.BlockSpec((1, tm, w), lambda i, j: (i, j, 0))
    out_shape = (
        jax.ShapeDtypeStruct((b, H_FOX, t, 256), BF16),
        jax.ShapeDtypeStruct((b, H_FOX // 2, t, 256), BF16),
        jax.ShapeDtypeStruct((b, H_FOX // 2, t, 256), BF16),
    )
    out_specs = (
        pl.BlockSpec((1, H_FOX, tm, 256), lambda i, j: (i, 0, j, 0)),
        pl.BlockSpec((1, H_FOX // 2, tm, 256), lambda i, j: (i, 0, j, 0)),
        pl.BlockSpec((1, H_FOX // 2, tm, 256), lambda i, j: (i, 0, j, 0)),
    )
    in_specs = [tok(FOX_W), tok(FOX_W), tok(FOX_W), tok(H_FOX),
                _const_spec((tm, tm)), _const_spec((3, H_FOX, LANES)), _const_spec((3, H_FOX, LANES)),
                _const_spec((1, LANES)), _const_spec((1, LANES))]
    return pl.pallas_call(
        _foxprep_kernel, grid=(b, t // tm), in_specs=in_specs, out_specs=out_specs, out_shape=out_shape,
        scratch_shapes=[pltpu.VMEM((1, H_FOX), F32)],
        compiler_params=_params(("parallel", "arbitrary")), name="foxprep",
    )(fq, fk, fv, lf, consts["tri"], consts["place_q"], consts["place_k"], consts["ones_q"], consts["ones_k"])


def _tile_pairs(t, tq, tk):
    qi, kj = [], []
    for i in range(t // tq):
        for j in range(((i + 1) * tq - 1) // tk + 1):
            qi.append(i)
            kj.append(j)
    return jnp.asarray(np.array(qi, np.int32)), jnp.asarray(np.array(kj, np.int32))


def _flash_step(s, v_ones, m_ref, acc_ref, idx=Ellipsis):
    m_old = m_ref[idx]
    m_new = jnp.maximum(m_old, jnp.max(s, axis=1, keepdims=True))
    alpha = jnp.exp(m_old - m_new)
    pr = jnp.exp(s - jnp.concatenate([m_new] * (s.shape[1] // LANES), axis=1))
    acc_ref[idx] = jnp.concatenate([alpha, alpha], axis=1) * acc_ref[idx] + _mm(pr.astype(BF16), v_ones)
    m_ref[idx] = m_new


def _mla_flash_kernel(qi_ref, kj_ref, q_ref, k_ref, v_ref, o_ref, m_ref, acc_ref, *, tq, tk):
    p = pl.program_id(1)
    qi = qi_ref[p]
    kj = kj_ref[p]
    rows = H_MLA * tq

    @pl.when(kj == 0)
    def _():
        m_ref[...] = jnp.full_like(m_ref, NEG_INF)
        acc_ref[...] = jnp.zeros_like(acc_ref)

    def update(masked):
        s = _mm_nt(q_ref[0].reshape(rows, 256), k_ref[0])
        if masked:
            row = qi * tq + lax.broadcasted_iota(jnp.int32, (rows, tk), 0) % tq
            col = kj * tk + lax.broadcasted_iota(jnp.int32, (rows, tk), 1)
            s = jnp.where(col <= row, s, NEG_INF)
        _flash_step(s, v_ref[0], m_ref, acc_ref)

    straddle = kj * tk + tk - 1 > qi * tq
    pl.when(straddle)(functools.partial(update, True))
    pl.when(jnp.logical_not(straddle))(functools.partial(update, False))

    @pl.when((kj + 1) * tk >= (qi + 1) * tq)
    def _():
        acc = acc_ref[...]
        o = acc[:, :KV_RANK] / acc[:, KV_RANK:]
        for hd in range(H_MLA):
            o_ref[0, :, hd * KV_RANK:(hd + 1) * KV_RANK] = o[hd * tq:(hd + 1) * tq]


def _mla_flash(qcat, kcat, vcat, tq, tk):
    b, _, t, _ = qcat.shape
    qi, kj = _tile_pairs(t, tq, tk)
    kv_spec = pl.BlockSpec((1, tk, 256), lambda i, p, qi, kj: (i, kj[p], 0))
    grid_spec = pltpu.PrefetchScalarGridSpec(
        num_scalar_prefetch=2, grid=(b, int(qi.shape[0])),
        in_specs=[pl.BlockSpec((1, H_MLA, tq, 256), lambda i, p, qi, kj: (i, 0, qi[p], 0)), kv_spec, kv_spec],
        out_specs=pl.BlockSpec((1, tq, H_MLA * KV_RANK), lambda i, p, qi, kj: (i, qi[p], 0)),
        scratch_shapes=[pltpu.VMEM((H_MLA * tq, LANES), F32), pltpu.VMEM((H_MLA * tq, 2 * KV_RANK), F32)])
    return pl.pallas_call(
        functools.partial(_mla_flash_kernel, tq=tq, tk=tk), grid_spec=grid_spec,
        out_shape=jax.ShapeDtypeStruct((b, t, H_MLA * KV_RANK), F32),
        compiler_params=_params(("parallel", "arbitrary")), name="mla_flash",
    )(qi, kj, qcat, kcat, vcat)


def _fox_flash_kernel(qi_ref, kj_ref, q_ref, k_ref, v_ref, o_ref, m_ref, acc_ref, *, t):
    p = pl.program_id(1)
    qi = qi_ref[p]
    kj = kj_ref[p]

    @pl.when(kj == 0)
    def _():
        m_ref[...] = jnp.full_like(m_ref, NEG_INF)
        acc_ref[...] = jnp.zeros_like(acc_ref)

    def heads(masked):
        def body(hd, carry):
            g = hd // 2
            s = _mm_nt(q_ref[0, hd], k_ref[0, g])
            if masked:
                row = lax.broadcasted_iota(jnp.int32, (t, t), 0)
                col = lax.broadcasted_iota(jnp.int32, (t, t), 1)
                s = jnp.where(col <= row, s, NEG_INF)
            _flash_step(s, v_ref[0, g], m_ref, acc_ref, hd)
            return carry
        lax.fori_loop(0, H_FOX, body, 0, unroll=2)

    pl.when(kj == qi)(functools.partial(heads, True))
    pl.when(kj != qi)(functools.partial(heads, False))

    @pl.when(kj == qi)
    def _():
        lane = lax.broadcasted_iota(jnp.int32, (1, LANES), 1)
        for g in range(H_FOX // 2):
            oa = acc_ref[2 * g, :, :LANES] / acc_ref[2 * g, :, LANES:]
            ob = acc_ref[2 * g + 1, :, :LANES] / acc_ref[2 * g + 1, :, LANES:]
            o_ref[0, :, g * LANES:(g + 1) * LANES] = jnp.where(lane < D_FOX, oa, ob)


def _fox_flash(qcat, kcat, vb, t_blk):
    b, _, t, _ = qcat.shape
    qi, kj = _tile_pairs(t, t_blk, t_blk)
    grid_spec = pltpu.PrefetchScalarGridSpec(
        num_scalar_prefetch=2, grid=(b, int(qi.shape[0])),
        in_specs=[pl.BlockSpec((1, H_FOX, t_blk, 256), lambda i, p, qi, kj: (i, 0, qi[p], 0)),
                  pl.BlockSpec((1, H_FOX // 2, t_blk, 256), lambda i, p, qi, kj: (i, 0, kj[p], 0)),
                  pl.BlockSpec((1, H_FOX // 2, t_blk, 256), lambda i, p, qi, kj: (i, 0, kj[p], 0))],
        out_specs=pl.BlockSpec((1, t_blk, FOX_W), lambda i, p, qi, kj: (i, qi[p], 0)),
        scratch_shapes=[pltpu.VMEM((H_FOX, t_blk, LANES), F32), pltpu.VMEM((H_FOX, t_blk, 2 * LANES), F32)])
    return pl.pallas_call(
        functools.partial(_fox_flash_kernel, t=t_blk), grid_spec=grid_spec,
        out_shape=jax.ShapeDtypeStruct((b, t, FOX_W), F32),
        compiler_params=_params(("parallel", "arbitrary")), name="fox_flash",
    )(qi, kj, qcat, kcat, vb)


def _softmax_step(s, m_ref, l_ref):
    m_old = m_ref[...]
    m_new = jnp.maximum(m_old, jnp.max(s, axis=1, keepdims=True))
    alpha = jnp.exp(m_old - m_new)
    pr = jnp.exp(s - m_new)
    l_ref[...] = alpha * l_ref[...] + jnp.sum(pr, axis=1, keepdims=True)
    m_ref[...] = m_new
    return pr, alpha


def _mla_decode_kernel(pt_ref, *refs, pps, nchunk, n_new):
    ckv_pages = refs[:pps]
    krt_pages = refs[pps:2 * pps]
    qa_ref, qr_ref, ckn_ref, krn_ref, o_ref, m_ref, l_ref, acc_ref = refs[2 * pps:]
    c = pl.program_id(1)

    @pl.when(c == 0)
    def _():
        m_ref[...] = jnp.full_like(m_ref, NEG_INF)
        l_ref[...] = jnp.zeros_like(l_ref)
        acc_ref[...] = jnp.zeros_like(acc_ref)

    qa = qa_ref[0]
    qr = qr_ref[0]

    def update(ckv, krt, mask):
        s = _mm_nt(qa, ckv) + _mm(qr, krt)
        if mask is not None:
            s = jnp.where(mask, s, NEG_INF)
        pr, alpha = _softmax_step(s, m_ref, l_ref)
        acc_ref[...] = alpha * acc_ref[...] + _mm(pr.astype(BF16), ckv)

    ckv = jnp.concatenate([r[0, 0].astype(BF16) for r in ckv_pages], axis=0)
    krt = jnp.concatenate([r[0, 0].astype(BF16) for r in krt_pages], axis=1)
    update(ckv, krt, None)

    @pl.when(c == nchunk - 1)
    def _():
        key = lax.broadcasted_iota(jnp.int32, (qa.shape[0], PAGE), 1)
        qry = lax.broadcasted_iota(jnp.int32, (qa.shape[0], PAGE), 0) % n_new
        update(ckn_ref[0].astype(BF16), krn_ref[0].astype(BF16), key <= qry)
        o_ref[0] = acc_ref[...] / l_ref[...]


def _mla_decode(layer, page_table, pool_ckv, pool_krt, qa, qr, ckn, krnt, pps):
    bs, npages = page_table.shape
    nchunk = npages // pps
    n_new = 8
    rows = H_MLA * n_new
    pt = page_table.reshape(-1)

    def page_spec(j, shape):
        return pl.BlockSpec((1, 1) + shape, lambda b, c, pt: (layer, pt[b * npages + c * pps + j], 0, 0))

    per_seq = lambda shape: pl.BlockSpec((1,) + shape, lambda b, c, pt: (b,) + (0,) * len(shape))
    in_specs = ([page_spec(j, (PAGE, KV_RANK)) for j in range(pps)]
                + [page_spec(j, (ROPE, PAGE)) for j in range(pps)]
                + [per_seq((rows, KV_RANK)), per_seq((rows, ROPE)), per_seq((PAGE, KV_RANK)),
                   per_seq((ROPE, PAGE))])
    grid_spec = pltpu.PrefetchScalarGridSpec(
        num_scalar_prefetch=1, grid=(bs, nchunk), in_specs=in_specs,
        out_specs=per_seq((rows, KV_RANK)),
        scratch_shapes=[pltpu.VMEM((rows, 1), F32), pltpu.VMEM((rows, 1), F32), pltpu.VMEM((rows, KV_RANK), F32)])
    return pl.pallas_call(
        functools.partial(_mla_decode_kernel, pps=pps, nchunk=nchunk, n_new=n_new), grid_spec=grid_spec,
        out_shape=jax.ShapeDtypeStruct((bs, rows, KV_RANK), F32),
        compiler_params=_params(("parallel", "arbitrary")), name="mla_decode",
    )(pt, *([pool_ckv] * pps), *([pool_krt] * pps), qa, qr, ckn, krnt)


def _fox_decode_kernel(pt_ref, *refs, pps, nchunk, n_new):
    kt_pages = refs[:pps]
    vt_pages = refs[pps:2 * pps]
    lf_pages = refs[2 * pps:3 * pps]
    (qbd_ref, triu_ref, knt_ref, vnt_ref, lfn_ref, o_ref, m_ref, l_ref, acc_ref, carry_ref) = refs[3 * pps:]
    c = pl.program_id(1)
    rows = n_new * H_FOX

    @pl.when(c == 0)
    def _():
        m_ref[...] = jnp.full_like(m_ref, NEG_INF)
        l_ref[...] = jnp.zeros_like(l_ref)
        acc_ref[...] = jnp.zeros_like(acc_ref)
        carry_ref[...] = jnp.zeros_like(carry_ref)

    qbd = qbd_ref[0]
    triu = triu_ref[...]

    def col_to_row(x):
        return jnp.transpose(jnp.broadcast_to(x, (rows, LANES)))[0:1, :]

    def update(kt, vt, lf_rows, mask):
        n = lf_rows.shape[0] // H_FOX
        cw = jnp.zeros(lf_rows.shape, F32)
        for part in _split3(lf_rows):
            cw = cw + _mm(part, triu)
        tot = jnp.sum(lf_rows, axis=1, keepdims=True)
        carry = carry_ref[...]
        bias = []
        for j in range(n):
            bias.append(cw[j * H_FOX:(j + 1) * H_FOX] + carry)
            carry = carry + tot[j * H_FOX:(j + 1) * H_FOX]
        carry_ref[...] = carry
        b = jnp.concatenate(bias, axis=1) if n > 1 else bias[0]
        s = _mm(qbd, kt)
        s = (s.reshape(n_new, H_FOX, s.shape[1]) - b[None]).reshape(rows, s.shape[1])
        if mask is not None:
            s = jnp.where(mask, s, NEG_INF)
        pr, alpha = _softmax_step(s, m_ref, l_ref)
        acc_ref[...] = col_to_row(alpha) * acc_ref[...] + _mm_nt(vt, pr.astype(BF16))

    kt = jnp.concatenate([r[0, 0].astype(BF16) for r in kt_pages], axis=1)
    vt = jnp.concatenate([r[0, 0].astype(BF16) for r in vt_pages], axis=1)
    lf = jnp.concatenate([r[0, 0] for r in lf_pages], axis=0)
    update(kt, vt, lf, None)

    @pl.when(c == nchunk - 1)
    def _():
        key = lax.broadcasted_iota(jnp.int32, (rows, PAGE), 1)
        qry = lax.broadcasted_iota(jnp.int32, (rows, PAGE), 0) // H_FOX
        update(knt_ref[0].astype(BF16), vnt_ref[0].astype(BF16), lfn_ref[0], key <= qry)
        lane = lax.broadcasted_iota(jnp.int32, (1, rows), 1) % H_FOX
        o = jnp.zeros((D_FOX, rows), F32)
        for hd in range(H_FOX):
            o = o + jnp.where(lane == hd, acc_ref[hd * D_FOX:(hd + 1) * D_FOX, :], 0.0)
        o_ref[0] = o / col_to_row(l_ref[...])


def _fox_decode(layer, page_table, pool_kt, pool_vt, pool_lft, qbd, knt, vnt, lfnt, consts, pps):
    bs, npages = page_table.shape
    nchunk = npages // pps
    n_new = 8
    rows = n_new * H_FOX
    pt = page_table.reshape(-1)

    def page_spec(j, shape):
        return pl.BlockSpec((1, 1) + shape, lambda b, c, pt: (layer, pt[b * npages + c * pps + j], 0, 0))

    per_seq = lambda shape: pl.BlockSpec((1,) + shape, lambda b, c, pt: (b,) + (0,) * len(shape))
    const = lambda shape: pl.BlockSpec(shape, lambda b, c, pt: (0,) * len(shape))
    in_specs = ([page_spec(j, (FOX_W, PAGE)) for j in range(pps)]
                + [page_spec(j, (FOX_W, PAGE)) for j in range(pps)]
                + [page_spec(j, (H_FOX, PAGE)) for j in range(pps)]
                + [per_seq((rows, FOX_W)), const((PAGE, PAGE)),
                   per_seq((FOX_W, PAGE)), per_seq((FOX_W, PAGE)), per_seq((H_FOX, PAGE))])
    grid_spec = pltpu.PrefetchScalarGridSpec(
        num_scalar_prefetch=1, grid=(bs, nchunk), in_specs=in_specs,
        out_specs=per_seq((D_FOX, rows)),
        scratch_shapes=[pltpu.VMEM((rows, 1), F32), pltpu.VMEM((rows, 1), F32),
                        pltpu.VMEM((FOX_W, rows), F32), pltpu.VMEM((H_FOX, 1), F32)])
    return pl.pallas_call(
        functools.partial(_fox_decode_kernel, pps=pps, nchunk=nchunk, n_new=n_new), grid_spec=grid_spec,
        out_shape=jax.ShapeDtypeStruct((bs, D_FOX, rows), F32),
        compiler_params=_params(("parallel", "arbitrary")), name="fox_decode",
    )(pt, *([pool_kt] * pps), *([pool_vt] * pps), *([pool_lft] * pps), qbd, consts["triu_page"],
      knt, vnt, lfnt)


def _outproj_kernel(x_ref, ol_ref, of_ref, wuv_ref, gm_ref, gf_ref, wo_ref, g1_ref, b1_ref, wmq_ref,
                    x1_ref, mq_ref):
    o_mla = _mm(ol_ref[0].astype(BF16), wuv_ref[...])
    nm = _rms(o_mla, gm_ref[...]).astype(BF16)
    nf = _rms(of_ref[0], gf_ref[...]).astype(BF16)
    mix = _mm(jnp.concatenate([nm, nf], axis=1), wo_ref[...])
    x1 = _ln(ALPHA * x_ref[0] + mix, g1_ref[...], b1_ref[...])
    x1_ref[0] = x1
    mq_ref[0] = _mm(x1.astype(BF16), wmq_ref[...]) * MEM_SCALE


def _outproj(x, o_lat, o_fox, lw, tm):
    b, t, d = x.shape
    tok = lambda w: pl.BlockSpec((1, tm, w), lambda i, j: (i, j, 0))
    in_specs = [tok(d), tok(H_MLA * KV_RANK), tok(FOX_W),
                _const_spec(lw["w_uv"].shape), _const_spec((1, FOX_W)), _const_spec((1, FOX_W)),
                _const_spec(lw["w_o"].shape), _const_spec((1, d)), _const_spec((1, d)),
                _const_spec(lw["w_mq"].shape)]
    return pl.pallas_call(
        _outproj_kernel, grid=(b, t // tm), in_specs=in_specs, out_specs=(tok(d), tok(d)),
        out_shape=(jax.ShapeDtypeStruct((b, t, d), F32), jax.ShapeDtypeStruct((b, t, d), F32)),
        compiler_params=_params(("parallel", "parallel")), name="outproj",
    )(x, o_lat, o_fox, lw["w_uv"], lw["g_mla_out"], lw["g_fox_out"], lw["w_o"], lw["ln1_g"], lw["ln1_b"],
      lw["w_mq"])


def _memkv_kernel(m_ref, wk_ref, wv_ref, k_ref, v_ref):
    mb = m_ref[...].astype(BF16)
    k_ref[...] = _mm(mb, wk_ref[...])
    v_ref[...] = _mm(mb, wv_ref[...])


def _memkv(mem, wk, wv):
    n, d = mem.shape
    return pl.pallas_call(
        _memkv_kernel, grid=(1,),
        in_specs=[_const_spec((n, d)), _const_spec(wk.shape), _const_spec(wv.shape)],
        out_specs=(_const_spec((n, d)), _const_spec((n, d))),
        out_shape=(jax.ShapeDtypeStruct((n, d), F32), jax.ShapeDtypeStruct((n, d), F32)),
        compiler_params=_params(("arbitrary",)), name="memkv",
    )(mem, wk, wv)


def _cross_kernel(mq_ref, mk_ref, mv_ref, o_ref):
    mq = mq_ref[0].astype(BF16)
    outs = []
    for hd in range(H_MEM):
        mk = mk_ref[0][:, hd * D_MEM:(hd + 1) * D_MEM]
        mv = mv_ref[0][:, hd * D_MEM:(hd + 1) * D_MEM]
        s = _mm_nt(mq[:, hd * D_MEM:(hd + 1) * D_MEM], mk.astype(BF16))
        pr = jnp.exp(s - jnp.max(s, axis=1, keepdims=True))
        pr = pr / jnp.sum(pr, axis=1, keepdims=True)
        outs.append(_mm(pr.astype(BF16), mv.astype(BF16)))
    o_ref[0] = jnp.concatenate(outs, axis=1)


def _cross(mq, mk, mv, tm):
    b, t, d = mq.shape
    tok = pl.BlockSpec((1, tm, d), lambda i, j: (i, j, 0))
    mem = pl.BlockSpec((1, MEM_TOK, d), lambda i, j: (i, 0, 0))
    return pl.pallas_call(
        _cross_kernel, grid=(b, t // tm), in_specs=[tok, mem, mem], out_specs=tok,
        out_shape=jax.ShapeDtypeStruct((b, t, d), F32),
        compiler_params=_params(("parallel", "parallel")), name="cross",
    )(mq, mk, mv)


def _cross_seq_kernel(mq_ref, mk_ref, mv_ref, o_ref):
    n_q = mq_ref.shape[1]
    cols = H_MEM * n_q
    mq = mq_ref[0]
    qs = jnp.concatenate([mq[:, hd * D_MEM:(hd + 1) * D_MEM] for hd in range(H_MEM)], axis=0).astype(BF16)
    kf = mk_ref[0, 0].reshape(MEM_TOK * H_MEM, D_MEM).astype(BF16)
    vf = mv_ref[0, 0].reshape(MEM_TOK * H_MEM, D_MEM).astype(BF16)
    s3 = _mm_nt(kf, qs).reshape(MEM_TOK * H_MEM // SUBLANES, SUBLANES, cols)
    sub = lax.broadcasted_iota(jnp.int32, (SUBLANES, cols), 0)
    lane = lax.broadcasted_iota(jnp.int32, (SUBLANES, cols), 1)
    diag = sub % H_MEM == lane // n_q
    mx = jnp.max(s3, axis=0)
    mx = jnp.maximum(mx, pltpu.roll(mx, H_MEM, 0))
    ex = jnp.where(diag[None], jnp.exp(s3 - mx[None]), 0.0)
    den = jnp.sum(ex, axis=0)
    den = den + pltpu.roll(den, H_MEM, 0)
    pr = jnp.where(diag[None], ex / den[None], 0.0).reshape(MEM_TOK * H_MEM, cols)
    o = _mm_tn(pr.astype(BF16), vf)
    o_ref[0] = jnp.concatenate([o[hd * n_q:(hd + 1) * n_q] for hd in range(H_MEM)], axis=1)


def _cross_seq(mq, mk, mv, layer):
    b, t, d = mq.shape
    tok = pl.BlockSpec((1, t, d), lambda i: (i, 0, 0))
    mem = pl.BlockSpec((1, 1, MEM_TOK, H_MEM, D_MEM), lambda i: (layer, i, 0, 0, 0))
    return pl.pallas_call(
        _cross_seq_kernel, grid=(b,), in_specs=[tok, mem, mem], out_specs=tok,
        out_shape=jax.ShapeDtypeStruct((b, t, d), F32),
        compiler_params=_params(("parallel",)), name="cross_seq",
    )(mq, mk, mv)


def _ffn_kernel(*refs, seq_mode, n_chunks):
    if seq_mode:
        (x_ref, mo_ref, wmo_ref, g2_ref, b2_ref, wup_ref, cw_ref, cb_ref, wdn_ref, g3_ref, b3_ref,
         y_ref, u_ref, carry_ref) = refs
    else:
        (x_ref, mo_ref, pre_ref, wmo_ref, g2_ref, b2_ref, wup_ref, cw_ref, cb_ref, wdn_ref, g3_ref, b3_ref,
         y_ref, u_ref) = refs
    tm = x_ref.shape[1]
    dff = wdn_ref.shape[0]
    fc = dff // n_chunks

    if seq_mode:
        @pl.when(pl.program_id(1) == 0)
        def _():
            carry_ref[...] = jnp.zeros_like(carry_ref)

    x2 = _ln(ALPHA * x_ref[0] + _mm(mo_ref[0].astype(BF16), wmo_ref[...]), g2_ref[...], b2_ref[...])
    xb = x2.astype(BF16)
    row = lax.broadcasted_iota(jnp.int32, (tm, 1), 0)

    def conv(lo):
        u = _mm(xb, wup_ref[:, lo:lo + fc])
        if seq_mode:
            ext = jnp.concatenate([carry_ref[:, lo:lo + fc], u], axis=0)
            u1 = pltpu.roll(ext, 1, 0)[SUBLANES:]
            u2 = pltpu.roll(ext, 2, 0)[SUBLANES:]
            carry_ref[:, lo:lo + fc] = u[tm - SUBLANES:]
            u_ref[0, 0, :, lo:lo + fc] = u[tm - SUBLANES:]
        else:
            e = pre_ref[0, :, lo:lo + fc]
            u1 = jnp.where(row % SUBLANES < 1, pltpu.roll(e, tm - 1, 0), pltpu.roll(u, 1, 0))
            u2 = jnp.where(row % SUBLANES < 2, e, pltpu.roll(u, 2, 0))
            u_ref[0, :, lo:lo + fc] = u
        cw = cw_ref[:, lo:lo + fc]
        return cb_ref[:, lo:lo + fc] + (cw[0:1] * u2 + cw[1:2] * u1 + cw[2:3] * u)

    ffn = jnp.zeros((tm, x_ref.shape[2]), F32)
    for ci in range(n_chunks):
        val = conv(ci * fc)
        gate = conv(dff + ci * fc)
        hcat = (gate * jax.nn.sigmoid(gate) * val).astype(BF16)
        ffn = ffn + _mm(hcat, wdn_ref[ci * fc:(ci + 1) * fc, :])
    y_ref[0] = _ln(ALPHA * x2 + ffn, g3_ref[...], b3_ref[...])


def _ffn(x1, mo, lw, tm, pre=None):
    b, t, d = x1.shape
    f2 = lw["w_up"].shape[1]
    seq_mode = pre is None
    tok = lambda w: pl.BlockSpec((1, tm, w), lambda i, j: (i, j, 0))
    consts = [_const_spec(lw["w_mo"].shape), _const_spec((1, d)), _const_spec((1, d)),
              _const_spec(lw["w_up"].shape), _const_spec((3, f2)), _const_spec((1, f2)),
              _const_spec(lw["w_down"].shape), _const_spec((1, d)), _const_spec((1, d))]
    cargs = (lw["w_mo"], lw["ln2_g"], lw["ln2_b"], lw["w_up"], lw["conv_w"], lw["conv_b"], lw["w_down"],
             lw["ln3_g"], lw["ln3_b"])
    if seq_mode:
        in_specs = [tok(d), tok(d)] + consts
        args = (x1, mo) + cargs
        u_shape = jax.ShapeDtypeStruct((b, t // tm, SUBLANES, f2), F32)
        u_spec = pl.BlockSpec((1, 1, SUBLANES, f2), lambda i, j: (i, j, 0, 0))
        scratch = [pltpu.VMEM((SUBLANES, f2), F32)]
    else:
        in_specs = [tok(d), tok(d), tok(f2)] + consts
        args = (x1, mo, pre) + cargs
        u_shape = jax.ShapeDtypeStruct((b, t, f2), F32)
        u_spec = tok(f2)
        scratch = []
    return pl.pallas_call(
        functools.partial(_ffn_kernel, seq_mode=seq_mode, n_chunks=2), grid=(b, t // tm),
        in_specs=in_specs, out_specs=(tok(d), u_spec),
        out_shape=(jax.ShapeDtypeStruct((b, t, d), F32), u_shape), scratch_shapes=scratch,
        compiler_params=_params(("parallel", "arbitrary")), name="ffn",
    )(*args)


def _layer_weights(l, w_in, b_forget, g_q_lat, g_kv_lat, w_uq, w_uk, w_uv, g_mla_out, g_fox_out, w_o, ln1_g,
                   ln1_b, w_mq, w_mk, w_mv, w_mo, ln2_g, ln2_b, w_up, conv_w, conv_b, w_down, ln3_g, ln3_b):
    wi = w_in[l]
    d = wi.shape[0]
    o_kr = Q_RANK + KV_RANK
    o_fq = o_kr + ROPE
    half = ROPE // 2
    kr = wi[:, o_kr:o_fq]
    kr_sw = jnp.concatenate([kr[:, half:], kr[:, :half]], axis=1)
    n_rep = LANES // ROPE
    w_aug = jnp.concatenate([
        wi[:, :o_kr], jnp.tile(kr, (1, n_rep)), jnp.tile(kr_sw, (1, n_rep)), wi[:, o_fq:o_fq + 3 * FOX_W],
        jnp.pad(wi[:, o_fq + 3 * FOX_W:], ((0, 0), (0, LANES - H_FOX)))], axis=1).astype(BF16)
    uq = w_uq[l]
    uq_rope = uq[:, :, NOPE:]
    uq_sw = jnp.concatenate([uq_rope[:, :, half:], uq_rope[:, :, :half]], axis=2)
    w_uq_aug = jnp.concatenate([uq[:, :, :NOPE].reshape(Q_RANK, -1), uq_rope.reshape(Q_RANK, -1),
                                uq_sw.reshape(Q_RANK, -1)], axis=1).astype(BF16)
    uk_t = jnp.transpose(w_uk[l], (1, 2, 0))
    parity = jax.nn.one_hot(jnp.arange(H_MLA) % 2, 2, dtype=F32)
    w_uk_pad = (parity[:, :, None, None] * uk_t[:, None]).reshape(H_MLA, 2 * NOPE, KV_RANK).astype(BF16)
    w_uv_bd = jnp.einsum("rhv,hg->hrgv", w_uv[l], jnp.eye(H_MLA, dtype=F32)).reshape(
        H_MLA * KV_RANK, H_MLA * V_DIM).astype(BF16)
    lane = jnp.arange(LANES)
    inv = ROPE_BASE ** (-jnp.arange(half, dtype=F32) / half)
    row = lambda v: v.reshape(1, -1).astype(F32)
    return dict(
        w_in=w_aug, b_forget=row(jnp.pad(b_forget[l], (0, LANES - H_FOX))), g_q=row(g_q_lat[l]),
        g_kv=row(g_kv_lat[l]), w_uq=w_uq_aug, w_uk=w_uk_pad, inv_freq=row(inv[lane % half]),
        rope_sign=row(jnp.where(lane % ROPE < half, -1.0, 1.0)),
        w_uv=w_uv_bd, g_mla_out=row(g_mla_out[l]), g_fox_out=row(g_fox_out[l]), w_o=w_o[l].astype(BF16),
        ln1_g=row(ln1_g[l]), ln1_b=row(ln1_b[l]), w_mq=w_mq[l].reshape(d, d).astype(BF16),
        w_mk=w_mk[l].reshape(d, d).astype(BF16), w_mv=w_mv[l].reshape(d, d).astype(BF16),
        w_mo=w_mo[l].reshape(d, d).astype(BF16), ln2_g=row(ln2_g[l]), ln2_b=row(ln2_b[l]),
        w_up=w_up[l].astype(BF16), conv_w=conv_w[l].astype(F32), conv_b=row(conv_b[l]),
        w_down=w_down[l].astype(BF16), ln3_g=row(ln3_g[l]), ln3_b=row(ln3_b[l]))


def _constants(tm):
    tri = (np.arange(tm)[:, None] >= np.arange(tm)[None, :]).astype(np.float32)
    place_q = np.zeros((3, H_FOX, LANES), np.float32)
    place_k = np.zeros((3, H_FOX, LANES), np.float32)
    ones_q = np.zeros((1, LANES), np.float32)
    ones_k = np.zeros((1, LANES), np.float32)
    for hd in range(H_FOX):
        for j in range(3):
            place_q[j, hd, hd * 8 + j] = 1.0
            place_k[j, hd, hd * 8 + 3 + j] = -1.0
            ones_q[0, hd * 8 + 3 + j] = 1.0
            ones_k[0, hd * 8 + j] = 1.0
    head_mask = (np.arange(FOX_W)[None, :] // D_FOX == np.arange(H_FOX)[:, None]).astype(np.float32)
    return dict(tri=jnp.asarray(tri, BF16), place_q=jnp.asarray(place_q, BF16),
                place_k=jnp.asarray(place_k, BF16), ones_q=jnp.asarray(ones_q), ones_k=jnp.asarray(ones_k),
                triu_page=jnp.asarray(tri[:PAGE, :PAGE].T, BF16), head_mask=jnp.asarray(head_mask, BF16))


def _tile(n, pref):
    return pref if n % pref == 0 else n


def _prompt_layer(x, pos, mem, lw, consts):
    b, t, d = x.shape
    tm = _tile(t, 512)
    qcat, kcat, vcat, ckv, kr, fq, fk, fv, lf = _inproj(x, pos, lw, tm)
    fqc, fkc, fvb = _foxprep(fq, fk, fv, lf, consts, tm)
    o_lat = _mla_flash(qcat, kcat, vcat, _tile(t, 256), tm)
    o_fox = _fox_flash(fqc, fkc, fvb, tm)
    x1, mq = _outproj(x, o_lat, o_fox, lw, tm)
    mk, mv = _memkv(mem.reshape(b * MEM_TOK, d), lw["w_mk"], lw["w_mv"])
    mk = mk.reshape(b, MEM_TOK, d)
    mv = mv.reshape(b, MEM_TOK, d)
    mo = _cross(mq, mk, mv, tm)
    y, u_tail = _ffn(x1, mo, lw, tm)
    state = (ckv, kr, fk.reshape(b, t, H_FOX, D_FOX), fv.reshape(b, t, H_FOX, D_FOX), lf,
             mk.reshape(b, MEM_TOK, H_MEM, D_MEM), mv.reshape(b, MEM_TOK, H_MEM, D_MEM),
             u_tail[:, -1, SUBLANES - 2:, :])
    return y, state


def _sample_layer(l, x, pos, caches, page_table, lw, consts):
    bs, ts, d = x.shape
    n = bs * ts
    tm = _tile(n, 512)
    c_ckv, c_kr, c_fk, c_fv, c_lf, c_mk, c_mv, c_conv = caches
    xf = x.reshape(1, n, d)
    qcat, _, _, ckv, kr, fq, fk, fv, lf = _inproj(xf, pos, lw, tm)
    q4 = qcat[0].reshape(H_MLA, bs, ts, 256)
    qa = jnp.transpose(q4[..., :KV_RANK], (1, 0, 2, 3)).reshape(bs, H_MLA * ts, KV_RANK)
    qr = jnp.stack([q4[hd, :, :, KV_RANK + (hd % 4) * ROPE:KV_RANK + (hd % 4 + 1) * ROPE]
                    for hd in range(H_MLA)], axis=1).reshape(bs, H_MLA * ts, ROPE)
    keys_last = lambda a: jnp.pad(jnp.swapaxes(a.reshape(bs, ts, -1), 1, 2), ((0, 0), (0, 0), (0, PAGE - ts)))
    npages = page_table.shape[1]
    ckn = jnp.pad(ckv.reshape(bs, ts, KV_RANK), ((0, 0), (0, PAGE - ts), (0, 0)))
    o_lat = _mla_decode(l, page_table, c_ckv, c_kr, qa, qr, ckn, keys_last(kr), _tile(npages, 32))
    o_lat = jnp.transpose(o_lat.reshape(bs, H_MLA, ts, KV_RANK), (0, 2, 1, 3)).reshape(1, n, H_MLA * KV_RANK)
    qbd = (fq.reshape(bs, ts, 1, FOX_W) * consts["head_mask"][None, None]).reshape(bs, ts * H_FOX, FOX_W)
    o_fox_t = _fox_decode(l, page_table, c_fk, c_fv, c_lf, qbd, keys_last(fk), keys_last(fv), keys_last(lf),
                          consts, _tile(npages, 16))
    o_fox = jnp.transpose(o_fox_t.reshape(bs, D_FOX, ts, H_FOX), (0, 2, 3, 1)).reshape(1, n, FOX_W)
    fk4 = fk.reshape(bs, ts, H_FOX, D_FOX)
    fv4 = fv.reshape(bs, ts, H_FOX, D_FOX)
    lf3 = lf.reshape(bs, ts, H_FOX)
    x1, mq = _outproj(xf, o_lat, o_fox, lw, tm)
    mo = _cross_seq(mq.reshape(bs, ts, d), c_mk, c_mv, l).reshape(1, n, d)
    pre = jnp.pad(c_conv[l], ((0, 0), (0, ts - c_conv.shape[2]), (0, 0))).reshape(1, n, -1)
    y, u = _ffn(x1, mo, lw, _tile(n, 256), pre=pre)
    state = (ckv.reshape(bs, ts, KV_RANK), kr.reshape(bs, ts, ROPE), fk4, fv4, lf3,
             u.reshape(bs, ts, -1)[:, ts - 2:, :])
    return y.reshape(bs, ts, d), state


def kernel(x_prompt, x_sample, mem_prompt, cache_mla_ckv, cache_mla_krope, cache_fox_k, cache_fox_v, cache_fox_logf, cache_mem_k, cache_mem_v, state_conv, page_table, w_in, b_forget, g_q_lat, g_kv_lat, w_uq, w_uk, w_uv, g_mla_out, g_fox_out, w_o, ln1_g, ln1_b, w_mq, w_mk, w_mv, w_mo, ln2_g, ln2_b, w_up, conv_w, conv_b, w_down, ln3_g, ln3_b):
    bp, tp, _ = x_prompt.shape
    bs, ts, _ = x_sample.shape
    n_past = page_table.shape[1] * PAGE
    pos_p = jnp.broadcast_to(jnp.arange(tp, dtype=F32)[None, :, None], (bp, tp, 1))
    pos_s = (n_past + jnp.arange(bs * ts) % ts).astype(F32).reshape(1, bs * ts, 1)
    consts = _constants(_tile(tp, 512))
    n_pool = cache_fox_k.shape[1]
    kv_view = lambda a: jnp.transpose(a, (0, 1, 3, 4, 2)).reshape(DEPTH, n_pool, FOX_W, PAGE)
    caches = (cache_mla_ckv, jnp.swapaxes(cache_mla_krope, 2, 3), kv_view(cache_fox_k), kv_view(cache_fox_v),
              jnp.swapaxes(cache_fox_logf, 2, 3), cache_mem_k, cache_mem_v, state_conv)
    weights = (w_in, b_forget, g_q_lat, g_kv_lat, w_uq, w_uk, w_uv, g_mla_out, g_fox_out, w_o, ln1_g, ln1_b,
               w_mq, w_mk, w_mv, w_mo, ln2_g, ln2_b, w_up, conv_w, conv_b, w_down, ln3_g, ln3_b)
    xp, xs = x_prompt, x_sample
    p_states, s_states = [], []
    for l in range(DEPTH):
        lw = _layer_weights(l, *weights)
        xp, st = _prompt_layer(xp, pos_p, mem_prompt, lw, consts)
        p_states.append(st)
        xs, st = _sample_layer(l, xs, pos_s, caches, page_table, lw, consts)
        s_states.append(st)
    stack = lambda states, i: jnp.stack([st[i] for st in states])
    return ((xp, xs) + tuple(stack(p_states, i) for i in range(8)) + tuple(stack(s_states, i) for i in range(6)))
```

```python
import functools

import numpy as np
import jax
import jax.numpy as jnp
from jax import lax
from jax.experimental import pallas as pl
from jax.experimental.pallas import tpu as pltpu

F32 = jnp.float32
BF16 = jnp.bfloat16

DEPTH = 2
PAGE = 128
H_MLA = 8
NOPE = 64
ROPE = 32
V_DIM = 64
Q_RANK = 256
KV_RANK = 128
MLA_SCALE = (NOPE + ROPE) ** -0.5
ROPE_BASE = 10000.0
H_FOX = 8
D_FOX = 64
FOX_W = H_FOX * D_FOX
FOX_SCALE = D_FOX ** -0.5
MEM_TOK = 256
H_MEM = 4
D_MEM = 256
MEM_SCALE = D_MEM ** -0.5
LN_EPS = 1e-5
RMS_EPS = 1e-6
ALPHA = (2 * DEPTH) ** 0.25

LANES = 128
SUBLANES = 8
VMEM_LIMIT = 56 * 1024 * 1024

C_Q, C_KV, C_KR, C_KS, C_FQ, C_FK, C_FV, C_FL, C_END = 0, 256, 384, 512, 640, 1152, 1664, 2176, 2304

NEG_INF = float("-inf")


def _mm(a, b):
    return jnp.dot(a, b, preferred_element_type=F32)


def _mm_nt(a, b):
    return lax.dot_general(a, b, (((1,), (1,)), ((), ())), preferred_element_type=F32)


def _mm_tn(a, b):
    return lax.dot_general(a, b, (((0,), (0,)), ((), ())), preferred_element_type=F32)


def _rms(x, g):
    return x * lax.rsqrt(jnp.mean(x * x, axis=-1, keepdims=True) + RMS_EPS) * g


def _ln(x, g, b):
    mu = jnp.mean(x, axis=-1, keepdims=True)
    xc = x - mu
    var = jnp.mean(xc * xc, axis=-1, keepdims=True)
    return xc * lax.rsqrt(var + LN_EPS) * g + b


def _split3(x):
    hi = x.astype(BF16)
    r = x - hi.astype(F32)
    mid = r.astype(BF16)
    lo = (r - mid.astype(F32)).astype(BF16)
    return hi, mid, lo


def _const_spec(shape):
    nd = len(shape)
    return pl.BlockSpec(shape, lambda *_: (0,) * nd)


def _params(sem):
    return pltpu.CompilerParams(dimension_semantics=sem, vmem_limit_bytes=VMEM_LIMIT)


def _inproj_kernel(x_ref, pos_ref, w_ref, bf_ref, gq_ref, gkv_ref, wuq_ref, wuk_ref, inv_ref, sgn_ref,
                   qcat_ref, kcat_ref, vcat_ref, ckv_ref, kr_ref, fq_ref, fk_ref, fv_ref, lf_ref):
    xb = x_ref[0].astype(BF16)
    h = _mm(xb, w_ref[...])
    ang = pos_ref[0] * inv_ref[...]
    cos = jnp.cos(ang)
    sin = jnp.sin(ang) * sgn_ref[...]
    qn = _rms(h[:, C_Q:C_KV], gq_ref[...]).astype(BF16)
    qu = _mm(qn, wuq_ref[...])
    ckv = _rms(h[:, C_KV:C_KR], gkv_ref[...])
    kr = h[:, C_KR:C_KS] * cos + h[:, C_KS:C_FQ] * sin
    ckv_ref[0] = ckv
    kr_ref[0] = kr[:, :ROPE]
    kcat_ref[0] = jnp.concatenate([ckv.astype(BF16), kr.astype(BF16)], axis=1)
    vcat_ref[0] = jnp.concatenate([ckv.astype(BF16), jnp.ones(ckv.shape, BF16)], axis=1)
    lane =lax.broadcasted_iota(jnp.int32, (1, LANES), 1)
    slabs = []
    for g in range(2):
        lo = 512 + g * LANES
        slabs.append(qu[:, lo:lo + LANES] * cos + qu[:, lo + 256:lo + 256 + LANES] * sin)
    for hd in range(H_MLA):
        g2 = hd // 2
        qabs = _mm(qu[:, g2 * LANES:(g2 + 1) * LANES].astype(BF16), wuk_ref[hd])
        slab = jnp.where(lane // ROPE == hd % 4, slabs[hd // 4], 0.0)
        qcat_ref[0, hd] = jnp.concatenate(
            [(qabs * MLA_SCALE).astype(BF16), (slab * MLA_SCALE).astype(BF16)], axis=1)
    fq_ref[0] = (h[:, C_FQ:C_FK] * FOX_SCALE).astype(BF16)
    fk_ref[0] = h[:, C_FK:C_FV]
    fv_ref[0] = h[:, C_FV:C_FL]
    z = h[:, C_FL:C_END] + bf_ref[...]
    logf = jnp.minimum(z, 0.0) - jnp.log1p(jnp.exp(-jnp.abs(z)))
    lf_ref[0] = logf[:, :H_FOX]


def _inproj(x, pos, lw, tm):
    b, t, d = x.shape
    grid = (b, t // tm)
    tok = lambda w: pl.BlockSpec((1, tm, w), lambda i, j: (i, j, 0))
    out_shape = (
        jax.ShapeDtypeStruct((b, H_MLA, t, 256), BF16),
        jax.ShapeDtypeStruct((b, t, 256), BF16),
        jax.ShapeDtypeStruct((b, t, 256), BF16),
        jax.ShapeDtypeStruct((b, t, KV_RANK), F32),
        jax.ShapeDtypeStruct((b, t, ROPE), F32),
        jax.ShapeDtypeStruct((b, t, FOX_W), BF16),
        jax.ShapeDtypeStruct((b, t, FOX_W), F32),
        jax.ShapeDtypeStruct((b, t, FOX_W), F32),
        jax.ShapeDtypeStruct((b, t, H_FOX), F32),
    )
    out_specs = (
        pl.BlockSpec((1, H_MLA, tm, 256), lambda i, j: (i, 0, j, 0)),
        tok(256), tok(256), tok(KV_RANK), tok(ROPE), tok(FOX_W), tok(FOX_W), tok(FOX_W), tok(H_FOX),
    )
    in_specs = [
        tok(d), tok(1),
        _const_spec(lw["w_in"].shape), _const_spec((1, LANES)), _const_spec((1, Q_RANK)),
        _const_spec((1, KV_RANK)), _const_spec(lw["w_uq"].shape), _const_spec(lw["w_uk"].shape),
        _const_spec((1, LANES)), _const_spec((1, LANES)),
    ]
    return pl.pallas_call(
        _inproj_kernel, grid=grid, in_specs=in_specs, out_specs=out_specs, out_shape=out_shape,
        compiler_params=_params(("parallel", "parallel")), name="inproj",
    )(x, pos, lw["w_in"], lw["b_forget"], lw["g_q"], lw["g_kv"], lw["w_uq"], lw["w_uk"],
      lw["inv_freq"], lw["rope_sign"])


def _foxprep_kernel(fq_ref, fk_ref, fv_ref, lf_ref, tri_ref, pq_ref, pk_ref, oq_ref, ok_ref,
                    qcat_ref, kcat_ref, vb_ref, carry_ref):
    tm = lf_ref.shape[1]

    @pl.when(pl.program_id(1) == 0)
    def _():
        carry_ref[...] = jnp.zeros_like(carry_ref)

    tri = tri_ref[...]
    c = carry_ref[...]
    for part in _split3(lf_ref[0]):
        c = c + _mm(tri, part)
    carry_ref[...] = c[tm - 1:tm, :]
    parts = _split3(c)
    augq = oq_ref[...]
    augk = ok_ref[...]
    for j in range(3):
        augq = augq + _mm(parts[j], pq_ref[j])
        augk = augk + _mm(parts[j], pk_ref[j])
    augk = augk.astype(BF16)
    lane = lax.broadcasted_iota(jnp.int32, (1, LANES), 1)
    fq = fq_ref[0]
    fkb = fk_ref[0].astype(BF16)
    fvb = fv_ref[0].astype(BF16)
    for g in range(H_FOX // 2):
        kcat_ref[0, g] = jnp.concatenate([fkb[:, g * LANES:(g + 1) * LANES], augk], axis=1)
        vb_ref[0, g] = jnp.concatenate([fvb[:, g * LANES:(g + 1) * LANES], jnp.ones((tm, LANES), BF16)], axis=1)
    for hd in range(H_FOX):
        g = hd // 2
        qs = jnp.where(lane // D_FOX == hd % 2, fq[:, g * LANES:(g + 1) * LANES].astype(F32), 0.0)
        aq = jnp.where(lane // 8 == hd, augq, 0.0)
        qcat_ref[0, hd] = jnp.concatenate([qs.astype(BF16), aq.astype(BF16)], axis=1)


def _foxprep(fq, fk, fv, lf, consts, tm):
    b, t, _ = fq.shape
    tok = lambda w: pl.BlockSpec((1, tm, w), lambda i, j: (i, j, 0))
    out_shape = (
        jax.ShapeDtypeStruct((b, H_FOX, t, 256), BF16),
        jax.ShapeDtypeStruct((b, H_FOX // 2, t, 256), BF16),
        jax.ShapeDtypeStruct((b, H_FOX // 2, t, 256), BF16),
    )
    out_specs = (
        pl.BlockSpec((1, H_FOX, tm, 256), lambda i, j: (i, 0, j, 0)),
        pl.BlockSpec((1, H_FOX // 2, tm, 256), lambda i, j: (i, 0, j, 0)),
        pl.BlockSpec((1, H_FOX // 2, tm, 256), lambda i, j: (i, 0, j, 0)),
    )
    in_specs = [tok(FOX_W), tok(FOX_W), tok(FOX_W), tok(H_FOX),
                _const_spec((tm, tm)), _const_spec((3, H_FOX, LANES)), _const_spec((3, H_FOX, LANES)),
                _const_spec((1, LANES)), _const_spec((1, LANES))]
    return pl.pallas_call(
        _foxprep_kernel, grid=(b, t // tm), in_specs=in_specs, out_specs=out_specs, out_shape=out_shape,
        scratch_shapes=[pltpu.VMEM((1, H_FOX), F32)],
        compiler_params=_params(("parallel", "arbitrary")), name="foxprep",
    )(fq, fk, fv, lf, consts["tri"], consts["place_q"], consts["place_k"], consts["ones_q"], consts["ones_k"])


def _tile_pairs(t, tq, tk):
    qi, kj = [], []
    for i in range(t // tq):
        for j in range(((i + 1) * tq - 1) // tk + 1):
            qi.append(i)
            kj.append(j)
    return jnp.asarray(np.array(qi, np.int32)), jnp.asarray(np.array(kj, np.int32))


def _flash_step(s, v_ones, m_ref, acc_ref, idx=Ellipsis):
    m_old = m_ref[idx]
    m_new = jnp.maximum(m_old, jnp.max(s, axis=1, keepdims=True))
    alpha = jnp.exp(m_old - m_new)
    pr = jnp.exp(s - jnp.concatenate([m_new] * (s.shape[1] // LANES), axis=1))
    acc_ref[idx] = jnp.concatenate([alpha, alpha], axis=1) * acc_ref[idx] + _mm(pr.astype(BF16), v_ones)
    m_ref[idx] = m_new


def _mla_flash_kernel(qi_ref, kj_ref, q_ref, k_ref, v_ref, o_ref, m_ref, acc_ref, *, tq, tk):
    p = pl.program_id(1)
    qi = qi_ref[p]
    kj = kj_ref[p]
    rows = H_MLA * tq

    @pl.when(kj == 0)
    def _():
        m_ref[...] = jnp.full_like(m_ref, NEG_INF)
        acc_ref[...] = jnp.zeros_like(acc_ref)

    def update(masked):
        s = _mm_nt(q_ref[0].reshape(rows, 256), k_ref[0])
        if masked:
            row = qi * tq + lax.broadcasted_iota(jnp.int32, (rows, tk), 0) % tq
            col = kj * tk + lax.broadcasted_iota(jnp.int32, (rows, tk), 1)
            s = jnp.where(col <= row, s, NEG_INF)
        _flash_step(s, v_ref[0], m_ref, acc_ref)

    straddle = kj * tk + tk - 1 > qi * tq
    pl.when(straddle)(functools.partial(update, True))
    pl.when(jnp.logical_not(straddle))(functools.partial(update, False))

    @pl.when((kj + 1) * tk >= (qi + 1) * tq)
    def _():
        acc = acc_ref[...]
        o = acc[:, :KV_RANK] / acc[:, KV_RANK:]
        for hd in range(H_MLA):
            o_ref[0, :, hd * KV_RANK:(hd + 1) * KV_RANK] = o[hd * tq:(hd + 1) * tq]


def _mla_flash(qcat, kcat, vcat, tq, tk):
    b, _, t, _ = qcat.shape
    qi, kj = _tile_pairs(t, tq, tk)
    kv_spec = pl.BlockSpec((1, tk, 256), lambda i, p, qi, kj: (i, kj[p], 0))
    grid_spec = pltpu.PrefetchScalarGridSpec(
        num_scalar_prefetch=2, grid=(b, int(qi.shape[0])),
        in_specs=[pl.BlockSpec((1, H_MLA, tq, 256), lambda i, p, qi, kj: (i, 0, qi[p], 0)), kv_spec, kv_spec],
        out_specs=pl.BlockSpec((1, tq, H_MLA * KV_RANK), lambda i, p, qi, kj: (i, qi[p], 0)),
        scratch_shapes=[pltpu.VMEM((H_MLA * tq, LANES), F32), pltpu.VMEM((H_MLA * tq, 2 * KV_RANK), F32)])
    return pl.pallas_call(
        functools.partial(_mla_flash_kernel, tq=tq, tk=tk), grid_spec=grid_spec,
        out_shape=jax.ShapeDtypeStruct((b, t, H_MLA * KV_RANK), F32),
        compiler_params=_params(("parallel", "arbitrary")), name="mla_flash",
    )(qi, kj, qcat, kcat, vcat)


def _fox_flash_kernel(qi_ref, kj_ref, q_ref, k_ref, v_ref, o_ref, m_ref, acc_ref, *, t):
    p = pl.program_id(1)
    qi = qi_ref[p]
    kj = kj_ref[p]

    @pl.when(kj == 0)
    def _():
        m_ref[...] = jnp.full_like(m_ref, NEG_INF)
        acc_ref[...] = jnp.zeros_like(acc_ref)

    def heads(masked):
        def body(g, carry):
            q2 = q_ref[0, pl.ds(2 * g, 2)].reshape(2 * t, 256)
            s = _mm_nt(q2, k_ref[0, g])
            if masked:
                row = lax.broadcasted_iota(jnp.int32, (2 * t, t), 0) % t
                col = lax.broadcasted_iota(jnp.int32, (2 * t, t), 1)
                s = jnp.where(col <= row, s, NEG_INF)
            _flash_step(s, v_ref[0, g], m_ref, acc_ref, g)
            return carry
        lax.fori_loop(0, H_FOX // 2, body, 0, unroll=2)

    pl.when(kj == qi)(functools.partial(heads, True))
    pl.when(kj != qi)(functools.partial(heads, False))

    @pl.when(kj == qi)
    def _():
        lane = lax.broadcasted_iota(jnp.int32, (1, LANES), 1)
        for g in range(H_FOX // 2):
            oa = acc_ref[g, :t, :LANES] / acc_ref[g, :t, LANES:]
            ob = acc_ref[g, t:, :LANES] / acc_ref[g, t:, LANES:]
            o_ref[0, :, g * LANES:(g + 1) * LANES] = jnp.where(lane < D_FOX, oa, ob)


def _fox_flash(qcat, kcat, vb, t_blk):
    b, _, t, _ = qcat.shape
    qi, kj = _tile_pairs(t, t_blk, t_blk)
    grid_spec = pltpu.PrefetchScalarGridSpec(
        num_scalar_prefetch=2, grid=(b, int(qi.shape[0])),
        in_specs=[pl.BlockSpec((1, H_FOX, t_blk, 256), lambda i, p, qi, kj: (i, 0, qi[p], 0)),
                  pl.BlockSpec((1, H_FOX // 2, t_blk, 256), lambda i, p, qi, kj: (i, 0, kj[p], 0)),
                  pl.BlockSpec((1, H_FOX // 2, t_blk, 256), lambda i, p, qi, kj: (i, 0, kj[p], 0))],
        out_specs=pl.BlockSpec((1, t_blk, FOX_W), lambda i, p, qi, kj: (i, qi[p], 0)),
        scratch_shapes=[pltpu.VMEM((H_FOX // 2, 2 * t_blk, LANES), F32),
                        pltpu.VMEM((H_FOX // 2, 2 * t_blk, 2 * LANES), F32)])
    return pl.pallas_call(
        functools.partial(_fox_flash_kernel, t=t_blk), grid_spec=grid_spec,
        out_shape=jax.ShapeDtypeStruct((b, t, FOX_W), F32),
        compiler_params=_params(("parallel", "arbitrary")), name="fox_flash",
    )(qi, kj, qcat, kcat, vb)


def _softmax_step(s, m_ref, l_ref):
    m_old = m_ref[...]
    m_new = jnp.maximum(m_old, jnp.max(s, axis=1, keepdims=True))
    alpha = jnp.exp(m_old - m_new)
    pr = jnp.exp(s - m_new)
    l_ref[...] = alpha * l_ref[...] + jnp.sum(pr, axis=1, keepdims=True)
    m_ref[...] = m_new
    return pr, alpha


def _mla_decode_phases(ckv_pages, krt_pages, qa_ref, qr_ref, ckn_ref, krn_ref, o_ref, m_ref, l_ref, acc_ref, *,
                       n_new):
    def init():
        m_ref[...] = jnp.full_like(m_ref, NEG_INF)
        l_ref[...] = jnp.zeros_like(l_ref)
        acc_ref[...] = jnp.zeros_like(acc_ref)

    def update(ckv, krt, mask):
        qa = qa_ref[0]
        s = _mm_nt(qa, ckv) + _mm(qr_ref[0], krt)
        if mask is not None:
            s = jnp.where(mask, s, NEG_INF)
        pr, alpha = _softmax_step(s, m_ref, l_ref)
        acc_ref[...] = alpha * acc_ref[...] + _mm(pr.astype(BF16), ckv)

    def pages():
        ckv = jnp.concatenate([r[...].astype(BF16) for r in ckv_pages], axis=0)
        krt = jnp.concatenate([r[...].astype(BF16) for r in krt_pages], axis=1)
        update(ckv, krt, None)

    def finish():
        rows = qa_ref.shape[1]
        key = lax.broadcasted_iota(jnp.int32, (rows, PAGE), 1)
        qry = lax.broadcasted_iota(jnp.int32, (rows, PAGE), 0) % n_new
        update(ckn_ref[0].astype(BF16), krn_ref[0].astype(BF16), key <= qry)
        o_ref[0] = acc_ref[...] / l_ref[...]

    return init, pages, finish


def _fox_decode_phases(kt_pages, vt_pages, lf_pages, qbd_ref, triu_ref, knt_ref, vnt_ref, lfn_ref, o_ref,
                       m_ref, l_ref, acc_ref, carry_ref, *, n_new):
    rows = n_new * H_FOX

    def init():
        m_ref[...] = jnp.full_like(m_ref, NEG_INF)
        l_ref[...] = jnp.zeros_like(l_ref)
        acc_ref[...] = jnp.zeros_like(acc_ref)
        carry_ref[...] = jnp.zeros_like(carry_ref)

    def col_to_row(x):
        return jnp.transpose(jnp.broadcast_to(x, (rows, LANES)))[0:1, :]

    def update(kt, vt, lf_rows, mask):
        qbd = qbd_ref[0]
        triu = triu_ref[...]
        n = lf_rows.shape[0] // H_FOX
        cw = jnp.zeros(lf_rows.shape, F32)
        for part in _split3(lf_rows):
            cw = cw + _mm(part, triu)
        tot = jnp.sum(lf_rows, axis=1, keepdims=True)
        carry = carry_ref[...]
        bias = []
        for j in range(n):
            bias.append(cw[j * H_FOX:(j + 1) * H_FOX] + carry)
            carry = carry + tot[j * H_FOX:(j + 1) * H_FOX]
        carry_ref[...] = carry
        b = jnp.concatenate(bias, axis=1) if n > 1 else bias[0]
        s = _mm(qbd, kt)
        s = (s.reshape(n_new, H_FOX, s.shape[1]) - b[None]).reshape(rows, s.shape[1])
        if mask is not None:
            s = jnp.where(mask, s, NEG_INF)
        pr, alpha = _softmax_step(s, m_ref, l_ref)
        acc_ref[...] = col_to_row(alpha) * acc_ref[...] + _mm_nt(vt, pr.astype(BF16))

    def pages():
        update(jnp.concatenate([r[...].astype(BF16) for r in kt_pages], axis=1),
               jnp.concatenate([r[...].astype(BF16) for r in vt_pages], axis=1),
               jnp.concatenate([r[...] for r in lf_pages], axis=0), None)

    def finish():
        key = lax.broadcasted_iota(jnp.int32, (rows, PAGE), 1)
        qry = lax.broadcasted_iota(jnp.int32, (rows, PAGE), 0) // H_FOX
        update(knt_ref[0].astype(BF16), vnt_ref[0].astype(BF16), lfn_ref[0], key <= qry)
        lane = lax.broadcasted_iota(jnp.int32, (1, rows), 1) % H_FOX
        o = jnp.zeros((D_FOX, rows), F32)
        for hd in range(H_FOX):
            o = o + jnp.where(lane == hd, acc_ref[hd * D_FOX:(hd + 1) * D_FOX, :], 0.0)
        o_ref[0] = o / col_to_row(l_ref[...])

    return init, pages, finish


def _decode_kernel(pt_ref, kt_hbm, vt_hbm, lf_hbm, ckv_hbm, krt_hbm, qbd_ref, triu_ref, knt_ref, vnt_ref, lfn_ref,
                   qa_ref, qr_ref, ckn_ref, krn_ref, of_ref, ol_ref, kt_buf, vt_buf, lf_buf, ckv_buf, krt_buf, sem,
                   fm_ref, fl_ref, facc_ref, carry_ref, mm_ref, ml_ref, macc_ref, *, layer, pps, nchunk, n_new):
    c = pl.program_id(1)
    step = pl.program_id(0) * nchunk + c
    last = pl.num_programs(0) * nchunk - 1
    slot = step % 2
    streams = ((kt_hbm, kt_buf), (vt_hbm, vt_buf), (lf_hbm, lf_buf), (ckv_hbm, ckv_buf), (krt_hbm, krt_buf))

    def copies(st, sl):
        out = []
        for j in range(pps):
            page = pt_ref[st * pps + j]
            for k, (hbm, buf) in enumerate(streams):
                out.append(pltpu.make_async_copy(hbm.at[layer, page], buf.at[sl, j], sem.at[sl, k]))
        return out

    @pl.when(step == 0)
    def _():
        for cp in copies(0, 0):
            cp.start()

    for cp in copies(step, slot):
        cp.wait()
    for cp in copies(jnp.minimum(step + 1, last), 1 - slot):
        cp.start()

    views = [[buf.at[slot, j] for j in range(pps)] for _, buf in streams]
    mla = _mla_decode_phases(views[3], views[4], qa_ref, qr_ref, ckn_ref, krn_ref, ol_ref, mm_ref, ml_ref,
                             macc_ref, n_new=n_new)
    fox = _fox_decode_phases(views[0], views[1], views[2], qbd_ref, triu_ref, knt_ref, vnt_ref, lfn_ref, of_ref,
                             fm_ref, fl_ref, facc_ref, carry_ref, n_new=n_new)

    @pl.when(c == 0)
    def _():
        mla[0]()
        fox[0]()

    mla[1]()
    fox[1]()

    @pl.when(c == nchunk - 1)
    def _():
        mla[2]()
        fox[2]()

    @pl.when(step == last)
    def _():
        for cp in copies(last, 1 - slot):
            cp.wait()


def _decode(layer, page_table, pools, fox_new, mla_new, consts, pps):
    pool_kt, pool_vt, pool_lft, pool_ckv, pool_krt = pools
    qbd, knt, vnt, lfnt = fox_new
    qa, qr, ckn, krnt = mla_new
    bs, npages = page_table.shape
    nchunk = npages // pps
    n_new = 8
    rows = n_new * H_FOX
    pt = page_table.reshape(-1)

    assert npages == nchunk * pps
    page_shapes = ((FOX_W, PAGE), (FOX_W, PAGE), (H_FOX, PAGE), (PAGE, KV_RANK), (ROPE, PAGE))
    hbm = pl.BlockSpec(memory_space=pl.ANY)
    per_seq = lambda shape: pl.BlockSpec((1,) + shape, lambda b, c, pt: (b,) + (0,) * len(shape))
    const = lambda shape: pl.BlockSpec(shape, lambda b, c, pt: (0,) * len(shape))
    in_specs = ([hbm] * len(page_shapes)
                + [per_seq((rows, FOX_W)), const((PAGE, PAGE)),
                   per_seq((FOX_W, PAGE)), per_seq((FOX_W, PAGE)), per_seq((H_FOX, PAGE)),
                   per_seq((rows, KV_RANK)), per_seq((rows, ROPE)), per_seq((PAGE, KV_RANK)),
                   per_seq((ROPE, PAGE))])
    n_slots = 2
    grid_spec = pltpu.PrefetchScalarGridSpec(
        num_scalar_prefetch=1, grid=(bs, nchunk), in_specs=in_specs,
        out_specs=(per_seq((D_FOX, rows)), per_seq((rows, KV_RANK))),
        scratch_shapes=([pltpu.VMEM((n_slots, pps) + shape, F32) for shape in page_shapes]
                        + [pltpu.SemaphoreType.DMA((n_slots, len(page_shapes))),
                           pltpu.VMEM((rows, 1), F32), pltpu.VMEM((rows, 1), F32),
                           pltpu.VMEM((FOX_W, rows), F32), pltpu.VMEM((H_FOX, 1), F32),
                           pltpu.VMEM((rows, 1), F32), pltpu.VMEM((rows, 1), F32),
                           pltpu.VMEM((rows, KV_RANK), F32)]))
    return pl.pallas_call(
        functools.partial(_decode_kernel, layer=layer, pps=pps, nchunk=nchunk, n_new=n_new), grid_spec=grid_spec,
        out_shape=(jax.ShapeDtypeStruct((bs, D_FOX, rows), F32), jax.ShapeDtypeStruct((bs, rows, KV_RANK), F32)),
        compiler_params=_params(("arbitrary", "arbitrary")), name="decode",
    )(pt, pool_kt, pool_vt, pool_lft, pool_ckv, pool_krt, qbd, consts["triu_page"], knt, vnt, lfnt, qa, qr,
      ckn, krnt)


def _outproj_kernel(x_ref, ol_ref, of_ref, wuv_ref, gm_ref, gf_ref, wo_ref, g1_ref, b1_ref, wmq_ref,
                    x1_ref, mq_ref):
    o_mla = _mm(ol_ref[0].astype(BF16), wuv_ref[...])
    nm = _rms(o_mla, gm_ref[...]).astype(BF16)
    nf = _rms(of_ref[0], gf_ref[...]).astype(BF16)
    mix = _mm(jnp.concatenate([nm, nf], axis=1), wo_ref[...])
    x1 = _ln(ALPHA * x_ref[0] + mix, g1_ref[...], b1_ref[...])
    x1_ref[0] = x1
    mq_ref[0] = _mm(x1.astype(BF16), wmq_ref[...]) * MEM_SCALE


def _outproj(x, o_lat, o_fox, lw, tm):
    b, t, d = x.shape
    tok = lambda w: pl.BlockSpec((1, tm, w), lambda i, j: (i, j, 0))
    in_specs = [tok(d), tok(H_MLA * KV_RANK), tok(FOX_W),
                _const_spec(lw["w_uv"].shape), _const_spec((1, FOX_W)), _const_spec((1, FOX_W)),
                _const_spec(lw["w_o"].shape), _const_spec((1, d)), _const_spec((1, d)),
                _const_spec(lw["w_mq"].shape)]
    return pl.pallas_call(
        _outproj_kernel, grid=(b, t // tm), in_specs=in_specs, out_specs=(tok(d), tok(d)),
        out_shape=(jax.ShapeDtypeStruct((b, t, d), F32), jax.ShapeDtypeStruct((b, t, d), F32)),
        compiler_params=_params(("parallel", "parallel")), name="outproj",
    )(x, o_lat, o_fox, lw["w_uv"], lw["g_mla_out"], lw["g_fox_out"], lw["w_o"], lw["ln1_g"], lw["ln1_b"],
      lw["w_mq"])


def _memkv_kernel(m_ref, wk_ref, wv_ref, k_ref, v_ref):
    mb = m_ref[...].astype(BF16)
    k_ref[...] = _mm(mb, wk_ref[...])
    v_ref[...] = _mm(mb, wv_ref[...])


def _memkv(mem, wk, wv):
    n, d = mem.shape
    return pl.pallas_call(
        _memkv_kernel, grid=(1,),
        in_specs=[_const_spec((n, d)), _const_spec(wk.shape), _const_spec(wv.shape)],
        out_specs=(_const_spec((n, d)), _const_spec((n, d))),
        out_shape=(jax.ShapeDtypeStruct((n, d), F32), jax.ShapeDtypeStruct((n, d), F32)),
        compiler_params=_params(("arbitrary",)), name="memkv",
    )(mem, wk, wv)


def _cross_kernel(mq_ref, mk_ref, mv_ref, o_ref):
    mq = mq_ref[0].astype(BF16)
    outs = []
    for hd in range(H_MEM):
        mk = mk_ref[0][:, hd * D_MEM:(hd + 1) * D_MEM]
        mv = mv_ref[0][:, hd * D_MEM:(hd + 1) * D_MEM]
        s = _mm_nt(mq[:, hd * D_MEM:(hd + 1) * D_MEM], mk.astype(BF16))
        pr = jnp.exp(s - jnp.max(s, axis=1, keepdims=True))
        pr = pr / jnp.sum(pr, axis=1, keepdims=True)
        outs.append(_mm(pr.astype(BF16), mv.astype(BF16)))
    o_ref[0] = jnp.concatenate(outs, axis=1)


def _cross(mq, mk, mv, tm):
    b, t, d = mq.shape
    tok = pl.BlockSpec((1, tm, d), lambda i, j: (i, j, 0))
    mem = pl.BlockSpec((1, MEM_TOK, d), lambda i, j: (i, 0, 0))
    return pl.pallas_call(
        _cross_kernel, grid=(b, t // tm), in_specs=[tok, mem, mem], out_specs=tok,
        out_shape=jax.ShapeDtypeStruct((b, t, d), F32),
        compiler_params=_params(("parallel", "parallel")), name="cross",
    )(mq, mk, mv)


def _cross_seq_kernel(mq_ref, mk_ref, mv_ref, o_ref):
    n_q = mq_ref.shape[1]
    cols = H_MEM * n_q
    mq = mq_ref[0]
    qs = jnp.concatenate([mq[:, hd * D_MEM:(hd + 1) * D_MEM] for hd in range(H_MEM)], axis=0).astype(BF16)
    kf = mk_ref[0, 0].reshape(MEM_TOK * H_MEM, D_MEM).astype(BF16)
    vf = mv_ref[0, 0].reshape(MEM_TOK * H_MEM, D_MEM).astype(BF16)
    s3 = _mm_nt(kf, qs).reshape(MEM_TOK * H_MEM // SUBLANES, SUBLANES, cols)
    sub = lax.broadcasted_iota(jnp.int32, (SUBLANES, cols), 0)
    lane = lax.broadcasted_iota(jnp.int32, (SUBLANES, cols), 1)
    diag = sub % H_MEM == lane // n_q
    mx = jnp.max(s3, axis=0)
    mx = jnp.maximum(mx, pltpu.roll(mx, H_MEM, 0))
    ex = jnp.where(diag[None], jnp.exp(s3 - mx[None]), 0.0)
    den = jnp.sum(ex, axis=0)
    den = den + pltpu.roll(den, H_MEM, 0)
    pr = jnp.where(diag[None], ex / den[None], 0.0).reshape(MEM_TOK * H_MEM, cols)
    o = _mm_tn(pr.astype(BF16), vf)
    o_ref[0] = jnp.concatenate([o[hd * n_q:(hd + 1) * n_q] for hd in range(H_MEM)], axis=1)


def _cross_seq(mq, mk, mv, layer):
    b, t, d = mq.shape
    tok = pl.BlockSpec((1, t, d), lambda i: (i, 0, 0))
    mem = pl.BlockSpec((1, 1, MEM_TOK, H_MEM, D_MEM), lambda i: (layer, i, 0, 0, 0))
    return pl.pallas_call(
        _cross_seq_kernel, grid=(b,), in_specs=[tok, mem, mem], out_specs=tok,
        out_shape=jax.ShapeDtypeStruct((b, t, d), F32),
        compiler_params=_params(("parallel",)), name="cross_seq",
    )(mq, mk, mv)


def _ffn_kernel(*refs, seq_mode, n_chunks):
    if seq_mode:
        (x_ref, mo_ref, wmo_ref, g2_ref, b2_ref, wup_ref, cw_ref, cb_ref, wdn_ref, g3_ref, b3_ref,
         y_ref, u_ref, carry_ref) = refs
    else:
        (x_ref, mo_ref, pre_ref, wmo_ref, g2_ref, b2_ref, wup_ref, cw_ref, cb_ref, wdn_ref, g3_ref, b3_ref,
         y_ref, u_ref) = refs
    tm = x_ref.shape[1]
    dff = wdn_ref.shape[0]
    fc = dff // n_chunks

    if seq_mode:
        @pl.when(pl.program_id(1) == 0)
        def _():
            carry_ref[...] = jnp.zeros_like(carry_ref)

    x2 = _ln(ALPHA * x_ref[0] + _mm(mo_ref[0].astype(BF16), wmo_ref[...]), g2_ref[...], b2_ref[...])
    xb = x2.astype(BF16)
    row = lax.broadcasted_iota(jnp.int32, (tm, 1), 0)

    def conv(lo):
        u = _mm(xb, wup_ref[:, lo:lo + fc])
        if seq_mode:
            ext = jnp.concatenate([carry_ref[:, lo:lo + fc], u], axis=0)
            u1 = pltpu.roll(ext, 1, 0)[SUBLANES:]
            u2 = pltpu.roll(ext, 2, 0)[SUBLANES:]
            carry_ref[:, lo:lo + fc] = u[tm - SUBLANES:]
            u_ref[0, 0, :, lo:lo + fc] = u[tm - SUBLANES:]
        else:
            e = pre_ref[0, :, lo:lo + fc]
            u1 = jnp.where(row % SUBLANES < 1, pltpu.roll(e, tm - 1, 0), pltpu.roll(u, 1, 0))
            u2 = jnp.where(row % SUBLANES < 2, e, pltpu.roll(u, 2, 0))
            u_ref[0, :, lo:lo + fc] = u
        cw = cw_ref[:, lo:lo + fc]
        return cb_ref[:, lo:lo + fc] + (cw[0:1] * u2 + cw[1:2] * u1 + cw[2:3] * u)

    ffn = jnp.zeros((tm, x_ref.shape[2]), F32)
    for ci in range(n_chunks):
        val = conv(ci * fc)
        gate = conv(dff + ci * fc)
        hcat = (gate * jax.nn.sigmoid(gate) * val).astype(BF16)
        ffn = ffn + _mm(hcat, wdn_ref[ci * fc:(ci + 1) * fc, :])
    y_ref[0] = _ln(ALPHA * x2 + ffn, g3_ref[...], b3_ref[...])


def _ffn(x1, mo, lw, tm, pre=None):
    b, t, d = x1.shape
    f2 = lw["w_up"].shape[1]
    seq_mode = pre is None
    tok = lambda w: pl.BlockSpec((1, tm, w), lambda i, j: (i, j, 0))
    consts = [_const_spec(lw["w_mo"].shape), _const_spec((1, d)), _const_spec((1, d)),
              _const_spec(lw["w_up"].shape), _const_spec((3, f2)), _const_spec((1, f2)),
              _const_spec(lw["w_down"].shape), _const_spec((1, d)), _const_spec((1, d))]
    cargs = (lw["w_mo"], lw["ln2_g"], lw["ln2_b"], lw["w_up"], lw["conv_w"], lw["conv_b"], lw["w_down"],
             lw["ln3_g"], lw["ln3_b"])
    if seq_mode:
        in_specs = [tok(d), tok(d)] + consts
        args = (x1, mo) + cargs
        u_shape = jax.ShapeDtypeStruct((b, t // tm, SUBLANES, f2), F32)
        u_spec = pl.BlockSpec((1, 1, SUBLANES, f2), lambda i, j: (i, j, 0, 0))
        scratch = [pltpu.VMEM((SUBLANES, f2), F32)]
    else:
        in_specs = [tok(d), tok(d), tok(f2)] + consts
        args = (x1, mo, pre) + cargs
        u_shape = jax.ShapeDtypeStruct((b, t, f2), F32)
        u_spec = tok(f2)
        scratch = []
    return pl.pallas_call(
        functools.partial(_ffn_kernel, seq_mode=seq_mode, n_chunks=2), grid=(b, t // tm),
        in_specs=in_specs, out_specs=(tok(d), u_spec),
        out_shape=(jax.ShapeDtypeStruct((b, t, d), F32), u_shape), scratch_shapes=scratch,
        compiler_params=_params(("parallel", "arbitrary")), name="ffn",
    )(*args)


def _layer_weights(l, w_in, b_forget, g_q_lat, g_kv_lat, w_uq, w_uk, w_uv, g_mla_out, g_fox_out, w_o, ln1_g,
                   ln1_b, w_mq, w_mk, w_mv, w_mo, ln2_g, ln2_b, w_up, conv_w, conv_b, w_down, ln3_g, ln3_b):
    wi = w_in[l]
    d = wi.shape[0]
    o_kr = Q_RANK + KV_RANK
    o_fq = o_kr + ROPE
    half = ROPE // 2
    kr = wi[:, o_kr:o_fq]
    kr_sw = jnp.concatenate([kr[:, half:], kr[:, :half]], axis=1)
    n_rep = LANES // ROPE
    w_aug = jnp.concatenate([
        wi[:, :o_kr], jnp.tile(kr, (1, n_rep)), jnp.tile(kr_sw, (1, n_rep)), wi[:, o_fq:o_fq + 3 * FOX_W],
        jnp.pad(wi[:, o_fq + 3 * FOX_W:], ((0, 0), (0, LANES - H_FOX)))], axis=1).astype(BF16)
    uq = w_uq[l]
    uq_rope = uq[:, :, NOPE:]
    uq_sw = jnp.concatenate([uq_rope[:, :, half:], uq_rope[:, :, :half]], axis=2)
    w_uq_aug = jnp.concatenate([uq[:, :, :NOPE].reshape(Q_RANK, -1), uq_rope.reshape(Q_RANK, -1),
                                uq_sw.reshape(Q_RANK, -1)], axis=1).astype(BF16)
    uk_t = jnp.transpose(w_uk[l], (1, 2, 0))
    parity = jax.nn.one_hot(jnp.arange(H_MLA) % 2, 2, dtype=F32)
    w_uk_pad = (parity[:, :, None, None] * uk_t[:, None]).reshape(H_MLA, 2 * NOPE, KV_RANK).astype(BF16)
    w_uv_bd = jnp.einsum("rhv,hg->hrgv", w_uv[l], jnp.eye(H_MLA, dtype=F32)).reshape(
        H_MLA * KV_RANK, H_MLA * V_DIM).astype(BF16)
    lane = jnp.arange(LANES)
    inv = ROPE_BASE ** (-jnp.arange(half, dtype=F32) / half)
    row = lambda v: v.reshape(1, -1).astype(F32)
    return dict(
        w_in=w_aug, b_forget=row(jnp.pad(b_forget[l], (0, LANES - H_FOX))), g_q=row(g_q_lat[l]),
        g_kv=row(g_kv_lat[l]), w_uq=w_uq_aug, w_uk=w_uk_pad, inv_freq=row(inv[lane % half]),
        rope_sign=row(jnp.where(lane % ROPE < half, -1.0, 1.0)),
        w_uv=w_uv_bd, g_mla_out=row(g_mla_out[l]), g_fox_out=row(g_fox_out[l]), w_o=w_o[l].astype(BF16),
        ln1_g=row(ln1_g[l]), ln1_b=row(ln1_b[l]), w_mq=w_mq[l].reshape(d, d).astype(BF16),
        w_mk=w_mk[l].reshape(d, d).astype(BF16), w_mv=w_mv[l].reshape(d, d).astype(BF16),
        w_mo=w_mo[l].reshape(d, d).astype(BF16), ln2_g=row(ln2_g[l]), ln2_b=row(ln2_b[l]),
        w_up=w_up[l].astype(BF16), conv_w=conv_w[l].astype(F32), conv_b=row(conv_b[l]),
        w_down=w_down[l].astype(BF16), ln3_g=row(ln3_g[l]), ln3_b=row(ln3_b[l]))


def _constants(tm):
    tri = (np.arange(tm)[:, None] >= np.arange(tm)[None, :]).astype(np.float32)
    place_q = np.zeros((3, H_FOX, LANES), np.float32)
    place_k = np.zeros((3, H_FOX, LANES), np.float32)
    ones_q = np.zeros((1, LANES), np.float32)
    ones_k = np.zeros((1, LANES), np.float32)
    for hd in range(H_FOX):
        for j in range(3):
            place_q[j, hd, hd * 8 + j] = 1.0
            place_k[j, hd, hd * 8 + 3 + j] = -1.0
            ones_q[0, hd * 8 + 3 + j] = 1.0
            ones_k[0, hd * 8 + j] = 1.0
    head_mask = (np.arange(FOX_W)[None, :] // D_FOX == np.arange(H_FOX)[:, None]).astype(np.float32)
    return dict(tri=jnp.asarray(tri, BF16), place_q=jnp.asarray(place_q, BF16),
                place_k=jnp.asarray(place_k, BF16), ones_q=jnp.asarray(ones_q), ones_k=jnp.asarray(ones_k),
                triu_page=jnp.asarray(tri[:PAGE, :PAGE].T, BF16), head_mask=jnp.asarray(head_mask, BF16))


TOKEN_TILE = 512
MLA_Q_TILE = 512
FFN_DEC_TILE = 256
DECODE_PAGES = 16


def _tile(n, pref):
    return pref if n % pref == 0 else n


def _prompt_layer(x, pos, mem, lw, consts):
    b, t, d = x.shape
    tm = _tile(t, TOKEN_TILE)
    qcat, kcat, vcat, ckv, kr, fq, fk, fv, lf = _inproj(x, pos, lw, tm)
    fqc, fkc, fvb = _foxprep(fq, fk, fv, lf, consts, tm)
    o_lat = _mla_flash(qcat, kcat, vcat, _tile(t, MLA_Q_TILE), tm)
    o_fox = _fox_flash(fqc, fkc, fvb, tm)
    x1, mq = _outproj(x, o_lat, o_fox, lw, tm)
    mk, mv = _memkv(mem.reshape(b * MEM_TOK, d), lw["w_mk"], lw["w_mv"])
    mk = mk.reshape(b, MEM_TOK, d)
    mv = mv.reshape(b, MEM_TOK, d)
    mo = _cross(mq, mk, mv, tm)
    y, u_tail = _ffn(x1, mo, lw, tm)
    state = (ckv, kr, fk.reshape(b, t, H_FOX, D_FOX), fv.reshape(b, t, H_FOX, D_FOX), lf,
             mk.reshape(b, MEM_TOK, H_MEM, D_MEM), mv.reshape(b, MEM_TOK, H_MEM, D_MEM),
             u_tail[:, -1, SUBLANES - 2:, :])
    return y, state


def _sample_layer(l, x, pos, caches, page_table, lw, consts):
    bs, ts, d = x.shape
    n = bs * ts
    tm = _tile(n, TOKEN_TILE)
    c_ckv, c_kr, c_fk, c_fv, c_lf, c_mk, c_mv, c_conv = caches
    xf = x.reshape(1, n, d)
    qcat, _, _, ckv, kr, fq, fk, fv, lf = _inproj(xf, pos, lw, tm)
    q4 = qcat[0].reshape(H_MLA, bs, ts, 256)
    qa = jnp.transpose(q4[..., :KV_RANK], (1, 0, 2, 3)).reshape(bs, H_MLA * ts, KV_RANK)
    qr = jnp.stack([q4[hd, :, :, KV_RANK + (hd % 4) * ROPE:KV_RANK + (hd % 4 + 1) * ROPE]
                    for hd in range(H_MLA)], axis=1).reshape(bs, H_MLA * ts, ROPE)
    keys_last = lambda a: jnp.pad(jnp.swapaxes(a.reshape(bs, ts, -1), 1, 2), ((0, 0), (0, 0), (0, PAGE - ts)))
    npages = page_table.shape[1]
    ckn = jnp.pad(ckv.reshape(bs, ts, KV_RANK), ((0, 0), (0, PAGE - ts), (0, 0)))
    qbd = (fq.reshape(bs, ts, 1, FOX_W) * consts["head_mask"][None, None]).reshape(bs, ts * H_FOX, FOX_W)
    o_fox_t, o_lat = _decode(l, page_table, (c_fk, c_fv, c_lf, c_ckv, c_kr),
                             (qbd, keys_last(fk), keys_last(fv), keys_last(lf)), (qa, qr, ckn, keys_last(kr)),
                             consts, _tile(npages, DECODE_PAGES))
    o_lat = jnp.transpose(o_lat.reshape(bs, H_MLA, ts, KV_RANK), (0, 2, 1, 3)).reshape(1, n, H_MLA * KV_RANK)
    o_fox = jnp.transpose(o_fox_t.reshape(bs, D_FOX, ts, H_FOX), (0, 2, 3, 1)).reshape(1, n, FOX_W)
    fk4 = fk.reshape(bs, ts, H_FOX, D_FOX)
    fv4 = fv.reshape(bs, ts, H_FOX, D_FOX)
    lf3 = lf.reshape(bs, ts, H_FOX)
    x1, mq = _outproj(xf, o_lat, o_fox, lw, tm)
    mo = _cross_seq(mq.reshape(bs, ts, d), c_mk, c_mv, l).reshape(1, n, d)
    pre = jnp.pad(c_conv[l], ((0, 0), (0, ts - c_conv.shape[2]), (0, 0))).reshape(1, n, -1)
    y, u = _ffn(x1, mo, lw, _tile(n, FFN_DEC_TILE), pre=pre)
    state = (ckv.reshape(bs, ts, KV_RANK), kr.reshape(bs, ts, ROPE), fk4, fv4, lf3,
             u.reshape(bs, ts, -1)[:, ts - 2:, :])
    return y.reshape(bs, ts, d), state


def kernel(x_prompt, x_sample, mem_prompt, cache_mla_ckv, cache_mla_krope, cache_fox_k, cache_fox_v, cache_fox_logf, cache_mem_k, cache_mem_v, state_conv, page_table, w_in, b_forget, g_q_lat, g_kv_lat, w_uq, w_uk, w_uv, g_mla_out, g_fox_out, w_o, ln1_g, ln1_b, w_mq, w_mk, w_mv, w_mo, ln2_g, ln2_b, w_up, conv_w, conv_b, w_down, ln3_g, ln3_b):
    bp, tp, _ = x_prompt.shape
    bs, ts, _ = x_sample.shape
    n_past = page_table.shape[1] * PAGE
    pos_p = jnp.broadcast_to(jnp.arange(tp, dtype=F32)[None, :, None], (bp, tp, 1))
    pos_s = (n_past + jnp.arange(bs * ts) % ts).astype(F32).reshape(1, bs * ts, 1)
    consts = _constants(_tile(tp, TOKEN_TILE))
    n_pool = cache_fox_k.shape[1]
    kv_view = lambda a: jnp.transpose(a, (0, 1, 3, 4, 2)).reshape(DEPTH, n_pool, FOX_W, PAGE)
    caches = (cache_mla_ckv, jnp.swapaxes(cache_mla_krope, 2, 3), kv_view(cache_fox_k), kv_view(cache_fox_v),
              jnp.swapaxes(cache_fox_logf, 2, 3), cache_mem_k, cache_mem_v, state_conv)
    weights = (w_in, b_forget, g_q_lat, g_kv_lat, w_uq, w_uk, w_uv, g_mla_out, g_fox_out, w_o, ln1_g, ln1_b,
               w_mq, w_mk, w_mv, w_mo, ln2_g, ln2_b, w_up, conv_w, conv_b, w_down, ln3_g, ln3_b)
    xp, xs = x_prompt, x_sample
    p_states, s_states = [], []
    for l in range(DEPTH):
        lw = _layer_weights(l, *weights)
        xp, st = _prompt_layer(xp, pos_p, mem_prompt, lw, consts)
        p_states.append(st)
        xs, st = _sample_layer(l, xs, pos_s, caches, page_table, lw, consts)
        s_states.append(st)
    stack = lambda states, i: jnp.stack([st[i] for st in states])
    return ((xp, xs) + tuple(stack(p_states, i) for i in range(8)) + tuple(stack(s_states, i) for i in range(6)))
```

```python
import functools

import numpy as np
import jax
import jax.numpy as jnp
from jax import lax
from jax.experimental import pallas as pl
from jax.experimental.pallas import tpu as pltpu

F32 = jnp.float32
BF16 = jnp.bfloat16

DEPTH = 2
PAGE = 128
H_MLA = 8
NOPE = 64
ROPE = 32
V_DIM = 64
Q_RANK = 256
KV_RANK = 128
MLA_SCALE = (NOPE + ROPE) ** -0.5
ROPE_BASE = 10000.0
H_FOX = 8
D_FOX = 64
FOX_W = H_FOX * D_FOX
FOX_SCALE = D_FOX ** -0.5
MEM_TOK = 256
H_MEM = 4
D_MEM = 256
MEM_SCALE = D_MEM ** -0.5
LN_EPS = 1e-5
RMS_EPS = 1e-6
ALPHA = (2 * DEPTH) ** 0.25

LANES = 128
SUBLANES = 8
VMEM_LIMIT = 56 * 1024 * 1024

C_Q, C_KV, C_KR, C_KS, C_FQ, C_FK, C_FV, C_FL, C_END = 0, 256, 384, 512, 640, 1152, 1664, 2176, 2304

NEG_INF = float("-inf")


def _mm(a, b):
    return jnp.dot(a, b, preferred_element_type=F32)


def _mm_nt(a, b):
    return lax.dot_general(a, b, (((1,), (1,)), ((), ())), preferred_element_type=F32)


def _mm_tn(a, b):
    return lax.dot_general(a, b, (((0,), (0,)), ((), ())), preferred_element_type=F32)


def _rms(x, g):
    return x * lax.rsqrt(jnp.mean(x * x, axis=-1, keepdims=True) + RMS_EPS) * g


def _ln(x, g, b):
    mu = jnp.mean(x, axis=-1, keepdims=True)
    xc = x - mu
    var = jnp.mean(xc * xc, axis=-1, keepdims=True)
    return xc * lax.rsqrt(var + LN_EPS) * g + b


def _split3(x):
    hi = x.astype(BF16)
    r = x - hi.astype(F32)
    mid = r.astype(BF16)
    lo = (r - mid.astype(F32)).astype(BF16)
    return hi, mid, lo


def _const_spec(shape):
    nd = len(shape)
    return pl.BlockSpec(shape, lambda *_: (0,) * nd)


def _params(sem):
    return pltpu.CompilerParams(dimension_semantics=sem, vmem_limit_bytes=VMEM_LIMIT)


def _inproj_kernel(x_ref, pos_ref, w_ref, bf_ref, gq_ref, gkv_ref, wuq_ref, wuk_ref, inv_ref, sgn_ref,
                   qcat_ref, kcat_ref, vcat_ref, ckv_ref, kr_ref, fq_ref, fk_ref, fv_ref, lf_ref):
    xb = x_ref[0].astype(BF16)
    h = _mm(xb, w_ref[...])
    ang = pos_ref[0] * inv_ref[...]
    cos = jnp.cos(ang)
    sin = jnp.sin(ang) * sgn_ref[...]
    qn = _rms(h[:, C_Q:C_KV], gq_ref[...]).astype(BF16)
    qu = _mm(qn, wuq_ref[...])
    ckv = _rms(h[:, C_KV:C_KR], gkv_ref[...])
    kr = h[:, C_KR:C_KS] * cos + h[:, C_KS:C_FQ] * sin
    ckv_ref[0] = ckv
    kr_ref[0] = kr[:, :ROPE]
    kcat_ref[0] = jnp.concatenate([ckv.astype(BF16), kr.astype(BF16)], axis=1)
    vcat_ref[0] = jnp.concatenate([ckv.astype(BF16), jnp.ones(ckv.shape, BF16)], axis=1)
    lane =lax.broadcasted_iota(jnp.int32, (1, LANES), 1)
    slabs = []
    for g in range(2):
        lo = 512 + g * LANES
        slabs.append(qu[:, lo:lo + LANES] * cos + qu[:, lo + 256:lo + 256 + LANES] * sin)
    for hd in range(H_MLA):
        g2 = hd // 2
        qabs = _mm(qu[:, g2 * LANES:(g2 + 1) * LANES].astype(BF16), wuk_ref[hd])
        slab = jnp.where(lane // ROPE == hd % 4, slabs[hd // 4], 0.0)
        qcat_ref[0, hd] = jnp.concatenate(
            [(qabs * MLA_SCALE).astype(BF16), (slab * MLA_SCALE).astype(BF16)], axis=1)
    fq_ref[0] = (h[:, C_FQ:C_FK] * FOX_SCALE).astype(BF16)
    fk_ref[0] = h[:, C_FK:C_FV]
    fv_ref[0] = h[:, C_FV:C_FL]
    z = h[:, C_FL:C_END] + bf_ref[...]
    logf = jnp.minimum(z, 0.0) - jnp.log1p(jnp.exp(-jnp.abs(z)))
    lf_ref[0] = logf[:, :H_FOX]


def _inproj(x, pos, lw, tm):
    b, t, d = x.shape
    grid = (b, t // tm)
    tok = lambda w: pl.BlockSpec((1, tm, w), lambda i, j: (i, j, 0))
    out_shape = (
        jax.ShapeDtypeStruct((b, H_MLA, t, 256), BF16),
        jax.ShapeDtypeStruct((b, t, 256), BF16),
        jax.ShapeDtypeStruct((b, t, 256), BF16),
        jax.ShapeDtypeStruct((b, t, KV_RANK), F32),
        jax.ShapeDtypeStruct((b, t, ROPE), F32),
        jax.ShapeDtypeStruct((b, t, FOX_W), BF16),
        jax.ShapeDtypeStruct((b, t, FOX_W), F32),
        jax.ShapeDtypeStruct((b, t, FOX_W), F32),
        jax.ShapeDtypeStruct((b, t, H_FOX), F32),
    )
    out_specs = (
        pl.BlockSpec((1, H_MLA, tm, 256), lambda i, j: (i, 0, j, 0)),
        tok(256), tok(256), tok(KV_RANK), tok(ROPE), tok(FOX_W), tok(FOX_W), tok(FOX_W), tok(H_FOX),
    )
    in_specs = [
        tok(d), tok(1),
        _const_spec(lw["w_in"].shape), _const_spec((1, LANES)), _const_spec((1, Q_RANK)),
        _const_spec((1, KV_RANK)), _const_spec(lw["w_uq"].shape), _const_spec(lw["w_uk"].shape),
        _const_spec((1, LANES)), _const_spec((1, LANES)),
    ]
    return pl.pallas_call(
        _inproj_kernel, grid=grid, in_specs=in_specs, out_specs=out_specs, out_shape=out_shape,
        compiler_params=_params(("parallel", "parallel")), name="inproj",
    )(x, pos, lw["w_in"], lw["b_forget"], lw["g_q"], lw["g_kv"], lw["w_uq"], lw["w_uk"],
      lw["inv_freq"], lw["rope_sign"])


def _foxprep_kernel(fq_ref, fk_ref, fv_ref, lf_ref, tri_ref, pq_ref, pk_ref, oq_ref, ok_ref,
                    qcat_ref, kcat_ref, vb_ref, carry_ref):
    tm = lf_ref.shape[1]

    @pl.when(pl.program_id(1) == 0)
    def _():
        carry_ref[...] = jnp.zeros_like(carry_ref)

    tri = tri_ref[...]
    c = carry_ref[...]
    for part in _split3(lf_ref[0]):
        c = c + _mm(tri, part)
    carry_ref[...] = c[tm - 1:tm, :]
    parts = _split3(c)
    augq = oq_ref[...]
    augk = ok_ref[...]
    for j in range(3):
        augq = augq + _mm(parts[j], pq_ref[j])
        augk = augk + _mm(parts[j], pk_ref[j])
    augk = augk.astype(BF16)
    lane = lax.broadcasted_iota(jnp.int32, (1, LANES), 1)
    fq = fq_ref[0]
    fkb = fk_ref[0].astype(BF16)
    fvb = fv_ref[0].astype(BF16)
    for g in range(H_FOX // 2):
        kcat_ref[0, g] = jnp.concatenate([fkb[:, g * LANES:(g + 1) * LANES], augk], axis=1)
        vb_ref[0, g] = jnp.concatenate([fvb[:, g * LANES:(g + 1) * LANES], jnp.ones((tm, LANES), BF16)], axis=1)
    for hd in range(H_FOX):
        g = hd // 2
        qs = jnp.where(lane // D_FOX == hd % 2, fq[:, g * LANES:(g + 1) * LANES].astype(F32), 0.0)
        aq = jnp.where(lane // 8 == hd, augq, 0.0)
        qcat_ref[0, hd] = jnp.concatenate([qs.astype(BF16), aq.astype(BF16)], axis=1)


def _foxprep(fq, fk, fv, lf, consts, tm):
    b, t, _ = fq.shape
    tok = lambda w: pl.BlockSpec((1, tm, w), lambda i, j: (i, j, 0))
    out_shape = (
        jax.ShapeDtypeStruct((b, H_FOX, t, 256), BF16),
        jax.ShapeDtypeStruct((b, H_FOX // 2, t, 256), BF16),
        jax.ShapeDtypeStruct((b, H_FOX // 2, t, 256), BF16),
    )
    out_specs = (
        pl.BlockSpec((1, H_FOX, tm, 256), lambda i, j: (i, 0, j, 0)),
        pl.BlockSpec((1, H_FOX // 2, tm, 256), lambda i, j: (i, 0, j, 0)),
        pl.BlockSpec((1, H_FOX // 2, tm, 256), lambda i, j: (i, 0, j, 0)),
    )
    in_specs = [tok(FOX_W), tok(FOX_W), tok(FOX_W), tok(H_FOX),
                _const_spec((tm, tm)), _const_spec((3, H_FOX, LANES)), _const_spec((3, H_FOX, LANES)),
                _const_spec((1, LANES)), _const_spec((1, LANES))]
    return pl.pallas_call(
        _foxprep_kernel, grid=(b, t // tm), in_specs=in_specs, out_specs=out_specs, out_shape=out_shape,
        scratch_shapes=[pltpu.VMEM((1, H_FOX), F32)],
        compiler_params=_params(("parallel", "arbitrary")), name="foxprep",
    )(fq, fk, fv, lf, consts["tri"], consts["place_q"], consts["place_k"], consts["ones_q"], consts["ones_k"])


def _tile_pairs(t, tq, tk):
    qi, kj = [], []
    for i in range(t // tq):
        for j in range(((i + 1) * tq - 1) // tk + 1):
            qi.append(i)
            kj.append(j)
    return jnp.asarray(np.array(qi, np.int32)), jnp.asarray(np.array(kj, np.int32))


def _flash_step(s, v_ones, m_ref, acc_ref, idx=Ellipsis):
    m_old = m_ref[idx]
    m_new = jnp.maximum(m_old, jnp.max(s, axis=1, keepdims=True))
    alpha = jnp.exp(m_old - m_new)
    pr = jnp.exp(s - jnp.concatenate([m_new] * (s.shape[1] // LANES), axis=1))
    acc_ref[idx] = jnp.concatenate([alpha, alpha], axis=1) * acc_ref[idx] + _mm(pr.astype(BF16), v_ones)
    m_ref[idx] = m_new


def _mla_flash_kernel(qi_ref, kj_ref, q_ref, k_ref, v_ref, o_ref, m_ref, acc_ref, *, tq, tk):
    p = pl.program_id(1)
    qi = qi_ref[p]
    kj = kj_ref[p]
    rows = H_MLA * tq

    @pl.when(kj == 0)
    def _():
        m_ref[...] = jnp.full_like(m_ref, NEG_INF)
        acc_ref[...] = jnp.zeros_like(acc_ref)

    def update(masked):
        s = _mm_nt(q_ref[0].reshape(rows, 256), k_ref[0])
        if masked:
            row = qi * tq + lax.broadcasted_iota(jnp.int32, (rows, tk), 0) % tq
            col = kj * tk + lax.broadcasted_iota(jnp.int32, (rows, tk), 1)
            s = jnp.where(col <= row, s, NEG_INF)
        _flash_step(s, v_ref[0], m_ref, acc_ref)

    straddle = kj * tk + tk - 1 > qi * tq
    pl.when(straddle)(functools.partial(update, True))
    pl.when(jnp.logical_not(straddle))(functools.partial(update, False))

    @pl.when((kj + 1) * tk >= (qi + 1) * tq)
    def _():
        acc = acc_ref[...]
        o = acc[:, :KV_RANK] / acc[:, KV_RANK:]
        for hd in range(H_MLA):
            o_ref[0, :, hd * KV_RANK:(hd + 1) * KV_RANK] = o[hd * tq:(hd + 1) * tq]


def _mla_flash(qcat, kcat, vcat, tq, tk):
    b, _, t, _ = qcat.shape
    qi, kj = _tile_pairs(t, tq, tk)
    kv_spec = pl.BlockSpec((1, tk, 256), lambda i, p, qi, kj: (i, kj[p], 0))
    grid_spec = pltpu.PrefetchScalarGridSpec(
        num_scalar_prefetch=2, grid=(b, int(qi.shape[0])),
        in_specs=[pl.BlockSpec((1, H_MLA, tq, 256), lambda i, p, qi, kj: (i, 0, qi[p], 0)), kv_spec, kv_spec],
        out_specs=pl.BlockSpec((1, tq, H_MLA * KV_RANK), lambda i, p, qi, kj: (i, qi[p], 0)),
        scratch_shapes=[pltpu.VMEM((H_MLA * tq, LANES), F32), pltpu.VMEM((H_MLA * tq, 2 * KV_RANK), F32)])
    return pl.pallas_call(
        functools.partial(_mla_flash_kernel, tq=tq, tk=tk), grid_spec=grid_spec,
        out_shape=jax.ShapeDtypeStruct((b, t, H_MLA * KV_RANK), F32),
        compiler_params=_params(("parallel", "arbitrary")), name="mla_flash",
    )(qi, kj, qcat, kcat, vcat)


def _fox_flash_kernel(qi_ref, kj_ref, q_ref, k_ref, v_ref, o_ref, m_ref, acc_ref, *, t):
    p = pl.program_id(1)
    qi = qi_ref[p]
    kj = kj_ref[p]

    @pl.when(kj == 0)
    def _():
        m_ref[...] = jnp.full_like(m_ref, NEG_INF)
        acc_ref[...] = jnp.zeros_like(acc_ref)

    def heads(masked):
        def body(g, carry):
            q2 = q_ref[0, pl.ds(2 * g, 2)].reshape(2 * t, 256)
            s = _mm_nt(q2, k_ref[0, g])
            if masked:
                row = lax.broadcasted_iota(jnp.int32, (2 * t, t), 0) % t
                col = lax.broadcasted_iota(jnp.int32, (2 * t, t), 1)
                s = jnp.where(col <= row, s, NEG_INF)
            _flash_step(s, v_ref[0, g], m_ref, acc_ref, g)
            return carry
        lax.fori_loop(0, H_FOX // 2, body, 0, unroll=2)

    pl.when(kj == qi)(functools.partial(heads, True))
    pl.when(kj != qi)(functools.partial(heads, False))

    @pl.when(kj == qi)
    def _():
        lane = lax.broadcasted_iota(jnp.int32, (1, LANES), 1)
        for g in range(H_FOX // 2):
            oa = acc_ref[g, :t, :LANES] / acc_ref[g, :t, LANES:]
            ob = acc_ref[g, t:, :LANES] / acc_ref[g, t:, LANES:]
            o_ref[0, :, g * LANES:(g + 1) * LANES] = jnp.where(lane < D_FOX, oa, ob)


def _fox_flash(qcat, kcat, vb, t_blk):
    b, _, t, _ = qcat.shape
    qi, kj = _tile_pairs(t, t_blk, t_blk)
    grid_spec = pltpu.PrefetchScalarGridSpec(
        num_scalar_prefetch=2, grid=(b, int(qi.shape[0])),
        in_specs=[pl.BlockSpec((1, H_FOX, t_blk, 256), lambda i, p, qi, kj: (i, 0, qi[p], 0)),
                  pl.BlockSpec((1, H_FOX // 2, t_blk, 256), lambda i, p, qi, kj: (i, 0, kj[p], 0)),
                  pl.BlockSpec((1, H_FOX // 2, t_blk, 256), lambda i, p, qi, kj: (i, 0, kj[p], 0))],
        out_specs=pl.BlockSpec((1, t_blk, FOX_W), lambda i, p, qi, kj: (i, qi[p], 0)),
        scratch_shapes=[pltpu.VMEM((H_FOX // 2, 2 * t_blk, LANES), F32),
                        pltpu.VMEM((H_FOX // 2, 2 * t_blk, 2 * LANES), F32)])
    return pl.pallas_call(
        functools.partial(_fox_flash_kernel, t=t_blk), grid_spec=grid_spec,
        out_shape=jax.ShapeDtypeStruct((b, t, FOX_W), F32),
        compiler_params=_params(("parallel", "arbitrary")), name="fox_flash",
    )(qi, kj, qcat, kcat, vb)


def _softmax_step(s, m_ref, l_ref):
    m_old = m_ref[...]
    m_new = jnp.maximum(m_old, jnp.max(s, axis=1, keepdims=True))
    alpha = jnp.exp(m_old - m_new)
    pr = jnp.exp(s - m_new)
    l_ref[...] = alpha * l_ref[...] + jnp.sum(pr, axis=1, keepdims=True)
    m_ref[...] = m_new
    return pr, alpha


def _mla_decode_phases(ckv_pages, krt_pages, qa_ref, qr_ref, ckn_ref, krn_ref, o_ref, m_ref, l_ref, acc_ref, *,
                       n_new):
    def init():
        m_ref[...] = jnp.full_like(m_ref, NEG_INF)
        l_ref[...] = jnp.zeros_like(l_ref)
        acc_ref[...] = jnp.zeros_like(acc_ref)

    def update(ckv, krt, mask):
        qa = qa_ref[0]
        s = _mm_nt(qa, ckv) + _mm(qr_ref[0], krt)
        if mask is not None:
            s = jnp.where(mask, s, NEG_INF)
        pr, alpha = _softmax_step(s, m_ref, l_ref)
        acc_ref[...] = alpha * acc_ref[...] + _mm(pr.astype(BF16), ckv)

    def pages():
        ckv = jnp.concatenate([r[...].astype(BF16) for r in ckv_pages], axis=0)
        krt = jnp.concatenate([r[...].astype(BF16) for r in krt_pages], axis=1)
        update(ckv, krt, None)

    def finish():
        rows = qa_ref.shape[1]
        key = lax.broadcasted_iota(jnp.int32, (rows, PAGE), 1)
        qry = lax.broadcasted_iota(jnp.int32, (rows, PAGE), 0) % n_new
        update(ckn_ref[0].astype(BF16), krn_ref[0].astype(BF16), key <= qry)
        o_ref[0] = acc_ref[...] / l_ref[...]

    return init, pages, finish


def _fox_decode_phases(kt_pages, vt_pages, lf_pages, qbd_ref, triu_ref, knt_ref, vnt_ref, lfn_ref, o_ref,
                       m_ref, l_ref, acc_ref, carry_ref, *, n_new):
    rows = n_new * H_FOX

    def init():
        m_ref[...] = jnp.full_like(m_ref, NEG_INF)
        l_ref[...] = jnp.zeros_like(l_ref)
        acc_ref[...] = jnp.zeros_like(acc_ref)
        carry_ref[...] = jnp.zeros_like(carry_ref)

    def col_to_row(x):
        return jnp.transpose(jnp.broadcast_to(x, (rows, LANES)))[0:1, :]

    def update(kt, vt, lf_rows, mask):
        qbd = qbd_ref[0]
        triu = triu_ref[...]
        n = lf_rows.shape[0] // H_FOX
        cw = jnp.zeros(lf_rows.shape, F32)
        for part in _split3(lf_rows):
            cw = cw + _mm(part, triu)
        tot = jnp.sum(lf_rows, axis=1, keepdims=True)
        carry = carry_ref[...]
        bias = []
        for j in range(n):
            bias.append(cw[j * H_FOX:(j + 1) * H_FOX] + carry)
            carry = carry + tot[j * H_FOX:(j + 1) * H_FOX]
        carry_ref[...] = carry
        b = jnp.concatenate(bias, axis=1) if n > 1 else bias[0]
        s = _mm(qbd, kt)
        s = (s.reshape(n_new, H_FOX, s.shape[1]) - b[None]).reshape(rows, s.shape[1])
        if mask is not None:
            s = jnp.where(mask, s, NEG_INF)
        pr, alpha = _softmax_step(s, m_ref, l_ref)
        acc_ref[...] = col_to_row(alpha) * acc_ref[...] + _mm_nt(vt, pr.astype(BF16))

    def pages():
        update(jnp.concatenate([r[...].astype(BF16) for r in kt_pages], axis=1),
               jnp.concatenate([r[...].astype(BF16) for r in vt_pages], axis=1),
               jnp.concatenate([r[...] for r in lf_pages], axis=0), None)

    def finish():
        key = lax.broadcasted_iota(jnp.int32, (rows, PAGE), 1)
        qry = lax.broadcasted_iota(jnp.int32, (rows, PAGE), 0) // H_FOX
        update(knt_ref[0].astype(BF16), vnt_ref[0].astype(BF16), lfn_ref[0], key <= qry)
        lane = lax.broadcasted_iota(jnp.int32, (1, rows), 1) % H_FOX
        o = jnp.zeros((D_FOX, rows), F32)
        for hd in range(H_FOX):
            o = o + jnp.where(lane == hd, acc_ref[hd * D_FOX:(hd + 1) * D_FOX, :], 0.0)
        o_ref[0] = o / col_to_row(l_ref[...])

    return init, pages, finish


def _decode_kernel(pt_ref, kt_hbm, vt_hbm, lf_hbm, ckv_hbm, krt_hbm, qbd_ref, triu_ref, knt_ref, vnt_ref, lfn_ref,
                   qa_ref, qr_ref, ckn_ref, krn_ref, of_ref, ol_ref, kt_buf, vt_buf, lf_buf, ckv_buf, krt_buf, sem,
                   fm_ref, fl_ref, facc_ref, carry_ref, mm_ref, ml_ref, macc_ref, *, layer, pps, nchunk, n_new):
    n_slots = kt_buf.shape[0]
    ahead = n_slots - 1
    c = pl.program_id(1)
    step = pl.program_id(0) * nchunk + c
    last = pl.num_programs(0) * nchunk - 1
    slot = step % n_slots
    streams = ((kt_hbm, kt_buf), (vt_hbm, vt_buf), (lf_hbm, lf_buf), (ckv_hbm, ckv_buf), (krt_hbm, krt_buf))

    def copies(v):
        st = jnp.minimum(v, last)
        sl = v % n_slots
        out = []
        for j in range(pps):
            page = pt_ref[st * pps + j]
            for k, (hbm, buf) in enumerate(streams):
                out.append(pltpu.make_async_copy(hbm.at[layer, page], buf.at[sl, j], sem.at[sl, k]))
        return out

    @pl.when(step == 0)
    def _():
        for v in range(ahead):
            for cp in copies(v):
                cp.start()

    for cp in copies(step):
        cp.wait()
    for cp in copies(step + ahead):
        cp.start()

    views = [[buf.at[slot, j] for j in range(pps)] for _, buf in streams]
    mla = _mla_decode_phases(views[3], views[4], qa_ref, qr_ref, ckn_ref, krn_ref, ol_ref, mm_ref, ml_ref,
                             macc_ref, n_new=n_new)
    fox = _fox_decode_phases(views[0], views[1], views[2], qbd_ref, triu_ref, knt_ref, vnt_ref, lfn_ref, of_ref,
                             fm_ref, fl_ref, facc_ref, carry_ref, n_new=n_new)

    @pl.when(c == 0)
    def _():
        mla[0]()
        fox[0]()

    mla[1]()
    fox[1]()

    @pl.when(c == nchunk - 1)
    def _():
        mla[2]()
        fox[2]()

    @pl.when(step == last)
    def _():
        for v in range(1, ahead + 1):
            for cp in copies(step + v):
                cp.wait()


def _decode(layer, page_table, pools, fox_new, mla_new, consts, pps):
    pool_kt, pool_vt, pool_lft, pool_ckv, pool_krt = pools
    qbd, knt, vnt, lfnt = fox_new
    qa, qr, ckn, krnt = mla_new
    bs, npages = page_table.shape
    nchunk = npages // pps
    n_new = 8
    rows = n_new * H_FOX
    pt = page_table.reshape(-1)

    assert npages == nchunk * pps
    page_shapes = ((FOX_W, PAGE), (FOX_W, PAGE), (H_FOX, PAGE), (PAGE, KV_RANK), (ROPE, PAGE))
    hbm = pl.BlockSpec(memory_space=pl.ANY)
    per_seq = lambda shape: pl.BlockSpec((1,) + shape, lambda b, c, pt: (b,) + (0,) * len(shape))
    const = lambda shape: pl.BlockSpec(shape, lambda b, c, pt: (0,) * len(shape))
    in_specs = ([hbm] * len(page_shapes)
                + [per_seq((rows, FOX_W)), const((PAGE, PAGE)),
                   per_seq((FOX_W, PAGE)), per_seq((FOX_W, PAGE)), per_seq((H_FOX, PAGE)),
                   per_seq((rows, KV_RANK)), per_seq((rows, ROPE)), per_seq((PAGE, KV_RANK)),
                   per_seq((ROPE, PAGE))])
    n_slots = 3
    grid_spec = pltpu.PrefetchScalarGridSpec(
        num_scalar_prefetch=1, grid=(bs, nchunk), in_specs=in_specs,
        out_specs=(per_seq((D_FOX, rows)), per_seq((rows, KV_RANK))),
        scratch_shapes=([pltpu.VMEM((n_slots, pps) + shape, F32) for shape in page_shapes]
                        + [pltpu.SemaphoreType.DMA((n_slots, len(page_shapes))),
                           pltpu.VMEM((rows, 1), F32), pltpu.VMEM((rows, 1), F32),
                           pltpu.VMEM((FOX_W, rows), F32), pltpu.VMEM((H_FOX, 1), F32),
                           pltpu.VMEM((rows, 1), F32), pltpu.VMEM((rows, 1), F32),
                           pltpu.VMEM((rows, KV_RANK), F32)]))
    return pl.pallas_call(
        functools.partial(_decode_kernel, layer=layer, pps=pps, nchunk=nchunk, n_new=n_new), grid_spec=grid_spec,
        out_shape=(jax.ShapeDtypeStruct((bs, D_FOX, rows), F32), jax.ShapeDtypeStruct((bs, rows, KV_RANK), F32)),
        compiler_params=_params(("arbitrary", "arbitrary")), name="decode",
    )(pt, pool_kt, pool_vt, pool_lft, pool_ckv, pool_krt, qbd, consts["triu_page"], knt, vnt, lfnt, qa, qr,
      ckn, krnt)


def _outproj_kernel(x_ref, ol_ref, of_ref, wuv_ref, gm_ref, gf_ref, wo_ref, g1_ref, b1_ref, wmq_ref,
                    x1_ref, mq_ref):
    o_mla = _mm(ol_ref[0].astype(BF16), wuv_ref[...])
    nm = _rms(o_mla, gm_ref[...]).astype(BF16)
    nf = _rms(of_ref[0], gf_ref[...]).astype(BF16)
    mix = _mm(jnp.concatenate([nm, nf], axis=1), wo_ref[...])
    x1 = _ln(ALPHA * x_ref[0] + mix, g1_ref[...], b1_ref[...])
    x1_ref[0] = x1
    mq_ref[0] = _mm(x1.astype(BF16), wmq_ref[...]) * MEM_SCALE


def _outproj(x, o_lat, o_fox, lw, tm):
    b, t, d = x.shape
    tok = lambda w: pl.BlockSpec((1, tm, w), lambda i, j: (i, j, 0))
    in_specs = [tok(d), tok(H_MLA * KV_RANK), tok(FOX_W),
                _const_spec(lw["w_uv"].shape), _const_spec((1, FOX_W)), _const_spec((1, FOX_W)),
                _const_spec(lw["w_o"].shape), _const_spec((1, d)), _const_spec((1, d)),
                _const_spec(lw["w_mq"].shape)]
    return pl.pallas_call(
        _outproj_kernel, grid=(b, t // tm), in_specs=in_specs, out_specs=(tok(d), tok(d)),
        out_shape=(jax.ShapeDtypeStruct((b, t, d), F32), jax.ShapeDtypeStruct((b, t, d), F32)),
        compiler_params=_params(("parallel", "parallel")), name="outproj",
    )(x, o_lat, o_fox, lw["w_uv"], lw["g_mla_out"], lw["g_fox_out"], lw["w_o"], lw["ln1_g"], lw["ln1_b"],
      lw["w_mq"])


def _memkv_kernel(m_ref, wk_ref, wv_ref, k_ref, v_ref):
    mb = m_ref[...].astype(BF16)
    k_ref[...] = _mm(mb, wk_ref[...])
    v_ref[...] = _mm(mb, wv_ref[...])


def _memkv(mem, wk, wv):
    n, d = mem.shape
    return pl.pallas_call(
        _memkv_kernel, grid=(1,),
        in_specs=[_const_spec((n, d)), _const_spec(wk.shape), _const_spec(wv.shape)],
        out_specs=(_const_spec((n, d)), _const_spec((n, d))),
        out_shape=(jax.ShapeDtypeStruct((n, d), F32), jax.ShapeDtypeStruct((n, d), F32)),
        compiler_params=_params(("arbitrary",)), name="memkv",
    )(mem, wk, wv)


def _cross_kernel(mq_ref, mk_ref, mv_ref, o_ref):
    mq = mq_ref[0].astype(BF16)
    outs = []
    for hd in range(H_MEM):
        mk = mk_ref[0][:, hd * D_MEM:(hd + 1) * D_MEM]
        mv = mv_ref[0][:, hd * D_MEM:(hd + 1) * D_MEM]
        s = _mm_nt(mq[:, hd * D_MEM:(hd + 1) * D_MEM], mk.astype(BF16))
        pr = jnp.exp(s - jnp.max(s, axis=1, keepdims=True))
        pr = pr / jnp.sum(pr, axis=1, keepdims=True)
        outs.append(_mm(pr.astype(BF16), mv.astype(BF16)))
    o_ref[0] = jnp.concatenate(outs, axis=1)


def _cross(mq, mk, mv, tm):
    b, t, d = mq.shape
    tok = pl.BlockSpec((1, tm, d), lambda i, j: (i, j, 0))
    mem = pl.BlockSpec((1, MEM_TOK, d), lambda i, j: (i, 0, 0))
    return pl.pallas_call(
        _cross_kernel, grid=(b, t // tm), in_specs=[tok, mem, mem], out_specs=tok,
        out_shape=jax.ShapeDtypeStruct((b, t, d), F32),
        compiler_params=_params(("parallel", "parallel")), name="cross",
    )(mq, mk, mv)


def _cross_seq_kernel(mq_ref, mk_ref, mv_ref, o_ref):
    n_q = mq_ref.shape[1]
    cols = H_MEM * n_q
    mq = mq_ref[0]
    qs = jnp.concatenate([mq[:, hd * D_MEM:(hd + 1) * D_MEM] for hd in range(H_MEM)], axis=0).astype(BF16)
    kf = mk_ref[0, 0].reshape(MEM_TOK * H_MEM, D_MEM).astype(BF16)
    vf = mv_ref[0, 0].reshape(MEM_TOK * H_MEM, D_MEM).astype(BF16)
    s3 = _mm_nt(kf, qs).reshape(MEM_TOK * H_MEM // SUBLANES, SUBLANES, cols)
    sub = lax.broadcasted_iota(jnp.int32, (SUBLANES, cols), 0)
    lane = lax.broadcasted_iota(jnp.int32, (SUBLANES, cols), 1)
    diag = sub % H_MEM == lane // n_q
    mx = jnp.max(s3, axis=0)
    mx = jnp.maximum(mx, pltpu.roll(mx, H_MEM, 0))
    ex = jnp.where(diag[None], jnp.exp(s3 - mx[None]), 0.0)
    den = jnp.sum(ex, axis=0)
    den = den + pltpu.roll(den, H_MEM, 0)
    pr = jnp.where(diag[None], ex / den[None], 0.0).reshape(MEM_TOK * H_MEM, cols)
    o = _mm_tn(pr.astype(BF16), vf)
    o_ref[0] = jnp.concatenate([o[hd * n_q:(hd + 1) * n_q] for hd in range(H_MEM)], axis=1)


def _cross_seq(mq, mk, mv, layer):
    b, t, d = mq.shape
    tok = pl.BlockSpec((1, t, d), lambda i: (i, 0, 0))
    mem = pl.BlockSpec((1, 1, MEM_TOK, H_MEM, D_MEM), lambda i: (layer, i, 0, 0, 0))
    return pl.pallas_call(
        _cross_seq_kernel, grid=(b,), in_specs=[tok, mem, mem], out_specs=tok,
        out_shape=jax.ShapeDtypeStruct((b, t, d), F32),
        compiler_params=_params(("parallel",)), name="cross_seq",
    )(mq, mk, mv)


def _ffn_kernel(*refs, seq_mode, n_chunks):
    if seq_mode:
        (x_ref, mo_ref, wmo_ref, g2_ref, b2_ref, wup_ref, cw_ref, cb_ref, wdn_ref, g3_ref, b3_ref,
         y_ref, u_ref, carry_ref) = refs
    else:
        (x_ref, mo_ref, pre_ref, wmo_ref, g2_ref, b2_ref, wup_ref, cw_ref, cb_ref, wdn_ref, g3_ref, b3_ref,
         y_ref, u_ref) = refs
    tm = x_ref.shape[1]
    dff = wdn_ref.shape[0]
    fc = dff // n_chunks

    if seq_mode:
        @pl.when(pl.program_id(1) == 0)
        def _():
            carry_ref[...] = jnp.zeros_like(carry_ref)

    x2 = _ln(ALPHA * x_ref[0] + _mm(mo_ref[0].astype(BF16), wmo_ref[...]), g2_ref[...], b2_ref[...])
    xb = x2.astype(BF16)
    row = lax.broadcasted_iota(jnp.int32, (tm, 1), 0)

    def conv(lo):
        u = _mm(xb, wup_ref[:, lo:lo + fc])
        if seq_mode:
            ext = jnp.concatenate([carry_ref[:, lo:lo + fc], u], axis=0)
            u1 = pltpu.roll(ext, 1, 0)[SUBLANES:]
            u2 = pltpu.roll(ext, 2, 0)[SUBLANES:]
            carry_ref[:, lo:lo + fc] = u[tm - SUBLANES:]
            u_ref[0, 0, :, lo:lo + fc] = u[tm - SUBLANES:]
        else:
            e = pre_ref[0, :, lo:lo + fc]
            u1 = jnp.where(row % SUBLANES < 1, pltpu.roll(e, tm - 1, 0), pltpu.roll(u, 1, 0))
            u2 = jnp.where(row % SUBLANES < 2, e, pltpu.roll(u, 2, 0))
            u_ref[0, :, lo:lo + fc] = u
        cw = cw_ref[:, lo:lo + fc]
        return cb_ref[:, lo:lo + fc] + (cw[0:1] * u2 + cw[1:2] * u1 + cw[2:3] * u)

    ffn = jnp.zeros((tm, x_ref.shape[2]), F32)
    for ci in range(n_chunks):
        val = conv(ci * fc)
        gate = conv(dff + ci * fc)
        hcat = (gate * jax.nn.sigmoid(gate) * val).astype(BF16)
        ffn = ffn + _mm(hcat, wdn_ref[ci * fc:(ci + 1) * fc, :])
    y_ref[0] = _ln(ALPHA * x2 + ffn, g3_ref[...], b3_ref[...])


def _ffn(x1, mo, lw, tm, pre=None):
    b, t, d = x1.shape
    f2 = lw["w_up"].shape[1]
    seq_mode = pre is None
    tok = lambda w: pl.BlockSpec((1, tm, w), lambda i, j: (i, j, 0))
    consts = [_const_spec(lw["w_mo"].shape), _const_spec((1, d)), _const_spec((1, d)),
              _const_spec(lw["w_up"].shape), _const_spec((3, f2)), _const_spec((1, f2)),
              _const_spec(lw["w_down"].shape), _const_spec((1, d)), _const_spec((1, d))]
    cargs = (lw["w_mo"], lw["ln2_g"], lw["ln2_b"], lw["w_up"], lw["conv_w"], lw["conv_b"], lw["w_down"],
             lw["ln3_g"], lw["ln3_b"])
    if seq_mode:
        in_specs = [tok(d), tok(d)] + consts
        args = (x1, mo) + cargs
        u_shape = jax.ShapeDtypeStruct((b, t // tm, SUBLANES, f2), F32)
        u_spec = pl.BlockSpec((1, 1, SUBLANES, f2), lambda i, j: (i, j, 0, 0))
        scratch = [pltpu.VMEM((SUBLANES, f2), F32)]
    else:
        in_specs = [tok(d), tok(d), tok(f2)] + consts
        args = (x1, mo, pre) + cargs
        u_shape = jax.ShapeDtypeStruct((b, t, f2), F32)
        u_spec = tok(f2)
        scratch = []
    return pl.pallas_call(
        functools.partial(_ffn_kernel, seq_mode=seq_mode, n_chunks=2), grid=(b, t // tm),
        in_specs=in_specs, out_specs=(tok(d), u_spec),
        out_shape=(jax.ShapeDtypeStruct((b, t, d), F32), u_shape), scratch_shapes=scratch,
        compiler_params=_params(("parallel", "arbitrary")), name="ffn",
    )(*args)


def _layer_weights(l, w_in, b_forget, g_q_lat, g_kv_lat, w_uq, w_uk, w_uv, g_mla_out, g_fox_out, w_o, ln1_g,
                   ln1_b, w_mq, w_mk, w_mv, w_mo, ln2_g, ln2_b, w_up, conv_w, conv_b, w_down, ln3_g, ln3_b):
    wi = w_in[l]
    d = wi.shape[0]
    o_kr = Q_RANK + KV_RANK
    o_fq = o_kr + ROPE
    half = ROPE // 2
    kr = wi[:, o_kr:o_fq]
    kr_sw = jnp.concatenate([kr[:, half:], kr[:, :half]], axis=1)
    n_rep = LANES // ROPE
    w_aug = jnp.concatenate([
        wi[:, :o_kr], jnp.tile(kr, (1, n_rep)), jnp.tile(kr_sw, (1, n_rep)), wi[:, o_fq:o_fq + 3 * FOX_W],
        jnp.pad(wi[:, o_fq + 3 * FOX_W:], ((0, 0), (0, LANES - H_FOX)))], axis=1).astype(BF16)
    uq = w_uq[l]
    uq_rope = uq[:, :, NOPE:]
    uq_sw = jnp.concatenate([uq_rope[:, :, half:], uq_rope[:, :, :half]], axis=2)
    w_uq_aug = jnp.concatenate([uq[:, :, :NOPE].reshape(Q_RANK, -1), uq_rope.reshape(Q_RANK, -1),
                                uq_sw.reshape(Q_RANK, -1)], axis=1).astype(BF16)
    uk_t = jnp.transpose(w_uk[l], (1, 2, 0))
    parity = jax.nn.one_hot(jnp.arange(H_MLA) % 2, 2, dtype=F32)
    w_uk_pad = (parity[:, :, None, None] * uk_t[:, None]).reshape(H_MLA, 2 * NOPE, KV_RANK).astype(BF16)
    w_uv_bd = jnp.einsum("rhv,hg->hrgv", w_uv[l], jnp.eye(H_MLA, dtype=F32)).reshape(
        H_MLA * KV_RANK, H_MLA * V_DIM).astype(BF16)
    lane = jnp.arange(LANES)
    inv = ROPE_BASE ** (-jnp.arange(half, dtype=F32) / half)
    row = lambda v: v.reshape(1, -1).astype(F32)
    return dict(
        w_in=w_aug, b_forget=row(jnp.pad(b_forget[l], (0, LANES - H_FOX))), g_q=row(g_q_lat[l]),
        g_kv=row(g_kv_lat[l]), w_uq=w_uq_aug, w_uk=w_uk_pad, inv_freq=row(inv[lane % half]),
        rope_sign=row(jnp.where(lane % ROPE < half, -1.0, 1.0)),
        w_uv=w_uv_bd, g_mla_out=row(g_mla_out[l]), g_fox_out=row(g_fox_out[l]), w_o=w_o[l].astype(BF16),
        ln1_g=row(ln1_g[l]), ln1_b=row(ln1_b[l]), w_mq=w_mq[l].reshape(d, d).astype(BF16),
        w_mk=w_mk[l].reshape(d, d).astype(BF16), w_mv=w_mv[l].reshape(d, d).astype(BF16),
        w_mo=w_mo[l].reshape(d, d).astype(BF16), ln2_g=row(ln2_g[l]), ln2_b=row(ln2_b[l]),
        w_up=w_up[l].astype(BF16), conv_w=conv_w[l].astype(F32), conv_b=row(conv_b[l]),
        w_down=w_down[l].astype(BF16), ln3_g=row(ln3_g[l]), ln3_b=row(ln3_b[l]))


def _constants(tm):
    tri = (np.arange(tm)[:, None] >= np.arange(tm)[None, :]).astype(np.float32)
    place_q = np.zeros((3, H_FOX, LANES), np.float32)
    place_k = np.zeros((3, H_FOX, LANES), np.float32)
    ones_q = np.zeros((1, LANES), np.float32)
    ones_k = np.zeros((1, LANES), np.float32)
    for hd in range(H_FOX):
        for j in range(3):
            place_q[j, hd, hd * 8 + j] = 1.0
            place_k[j, hd, hd * 8 + 3 + j] = -1.0
            ones_q[0, hd * 8 + 3 + j] = 1.0
            ones_k[0, hd * 8 + j] = 1.0
    head_mask = (np.arange(FOX_W)[None, :] // D_FOX == np.arange(H_FOX)[:, None]).astype(np.float32)
    return dict(tri=jnp.asarray(tri, BF16), place_q=jnp.asarray(place_q, BF16),
                place_k=jnp.asarray(place_k, BF16), ones_q=jnp.asarray(ones_q), ones_k=jnp.asarray(ones_k),
                triu_page=jnp.asarray(tri[:PAGE, :PAGE].T, BF16), head_mask=jnp.asarray(head_mask, BF16))


TOKEN_TILE = 512
MLA_Q_TILE = 512
FFN_DEC_TILE = 256
DECODE_PAGES = 16


def _tile(n, pref):
    return pref if n % pref == 0 else n


def _prompt_layer(x, pos, mem, lw, consts):
    b, t, d = x.shape
    tm = _tile(t, TOKEN_TILE)
    qcat, kcat, vcat, ckv, kr, fq, fk, fv, lf = _inproj(x, pos, lw, tm)
    fqc, fkc, fvb = _foxprep(fq, fk, fv, lf, consts, tm)
    o_lat = _mla_flash(qcat, kcat, vcat, _tile(t, MLA_Q_TILE), tm)
    o_fox = _fox_flash(fqc, fkc, fvb, tm)
    x1, mq = _outproj(x, o_lat, o_fox, lw, tm)
    mk, mv = _memkv(mem.reshape(b * MEM_TOK, d), lw["w_mk"], lw["w_mv"])
    mk = mk.reshape(b, MEM_TOK, d)
    mv = mv.reshape(b, MEM_TOK, d)
    mo = _cross(mq, mk, mv, tm)
    y, u_tail = _ffn(x1, mo, lw, tm)
    state = (ckv, kr, fk.reshape(b, t, H_FOX, D_FOX), fv.reshape(b, t, H_FOX, D_FOX), lf,
             mk.reshape(b, MEM_TOK, H_MEM, D_MEM), mv.reshape(b, MEM_TOK, H_MEM, D_MEM),
             u_tail[:, -1, SUBLANES - 2:, :])
    return y, state


def _sample_layer(l, x, pos, caches, page_table, lw, consts):
    bs, ts, d = x.shape
    n = bs * ts
    tm = _tile(n, TOKEN_TILE)
    c_ckv, c_kr, c_fk, c_fv, c_lf, c_mk, c_mv, c_conv = caches
    xf = x.reshape(1, n, d)
    qcat, _, _, ckv, kr, fq, fk, fv, lf = _inproj(xf, pos, lw, tm)
    q4 = qcat[0].reshape(H_MLA, bs, ts, 256)
    qa = jnp.transpose(q4[..., :KV_RANK], (1, 0, 2, 3)).reshape(bs, H_MLA * ts, KV_RANK)
    qr = jnp.stack([q4[hd, :, :, KV_RANK + (hd % 4) * ROPE:KV_RANK + (hd % 4 + 1) * ROPE]
                    for hd in range(H_MLA)], axis=1).reshape(bs, H_MLA * ts, ROPE)
    keys_last = lambda a: jnp.pad(jnp.swapaxes(a.reshape(bs, ts, -1), 1, 2), ((0, 0), (0, 0), (0, PAGE - ts)))
    npages = page_table.shape[1]
    ckn = jnp.pad(ckv.reshape(bs, ts, KV_RANK), ((0, 0), (0, PAGE - ts), (0, 0)))
    qbd = (fq.reshape(bs, ts, 1, FOX_W) * consts["head_mask"][None, None]).reshape(bs, ts * H_FOX, FOX_W)
    o_fox_t, o_lat = _decode(l, page_table, (c_fk, c_fv, c_lf, c_ckv, c_kr),
                             (qbd, keys_last(fk), keys_last(fv), keys_last(lf)), (qa, qr, ckn, keys_last(kr)),
                             consts, _tile(npages, DECODE_PAGES))
    o_lat = jnp.transpose(o_lat.reshape(bs, H_MLA, ts, KV_RANK), (0, 2, 1, 3)).reshape(1, n, H_MLA * KV_RANK)
    o_fox = jnp.transpose(o_fox_t.reshape(bs, D_FOX, ts, H_FOX), (0, 2, 3, 1)).reshape(1, n, FOX_W)
    fk4 = fk.reshape(bs, ts, H_FOX, D_FOX)
    fv4 = fv.reshape(bs, ts, H_FOX, D_FOX)
    lf3 = lf.reshape(bs, ts, H_FOX)
    x1, mq = _outproj(xf, o_lat, o_fox, lw, tm)
    mo = _cross_seq(mq.reshape(bs, ts, d), c_mk, c_mv, l).reshape(1, n, d)
    pre = jnp.pad(c_conv[l], ((0, 0), (0, ts - c_conv.shape[2]), (0, 0))).reshape(1, n, -1)
    y, u = _ffn(x1, mo, lw, _tile(n, FFN_DEC_TILE), pre=pre)
    state = (ckv.reshape(bs, ts, KV_RANK), kr.reshape(bs, ts, ROPE), fk4, fv4, lf3,
             u.reshape(bs, ts, -1)[:, ts - 2:, :])
    return y.reshape(bs, ts, d), state


def kernel(x_prompt, x_sample, mem_prompt, cache_mla_ckv, cache_mla_krope, cache_fox_k, cache_fox_v, cache_fox_logf, cache_mem_k, cache_mem_v, state_conv, page_table, w_in, b_forget, g_q_lat, g_kv_lat, w_uq, w_uk, w_uv, g_mla_out, g_fox_out, w_o, ln1_g, ln1_b, w_mq, w_mk, w_mv, w_mo, ln2_g, ln2_b, w_up, conv_w, conv_b, w_down, ln3_g, ln3_b):
    bp, tp, _ = x_prompt.shape
    bs, ts, _ = x_sample.shape
    n_past = page_table.shape[1] * PAGE
    pos_p = jnp.broadcast_to(jnp.arange(tp, dtype=F32)[None, :, None], (bp, tp, 1))
    pos_s = (n_past + jnp.arange(bs * ts) % ts).astype(F32).reshape(1, bs * ts, 1)
    consts = _constants(_tile(tp, TOKEN_TILE))
    n_pool = cache_fox_k.shape[1]
    kv_view = lambda a: jnp.transpose(a, (0, 1, 3, 4, 2)).reshape(DEPTH, n_pool, FOX_W, PAGE)
    caches = (cache_mla_ckv, jnp.swapaxes(cache_mla_krope, 2, 3), kv_view(cache_fox_k), kv_view(cache_fox_v),
              jnp.swapaxes(cache_fox_logf, 2, 3), cache_mem_k, cache_mem_v, state_conv)
    weights = (w_in, b_forget, g_q_lat, g_kv_lat, w_uq, w_uk, w_uv, g_mla_out, g_fox_out, w_o, ln1_g, ln1_b,
               w_mq, w_mk, w_mv, w_mo, ln2_g, ln2_b, w_up, conv_w, conv_b, w_down, ln3_g, ln3_b)
    xp, xs = x_prompt, x_sample
    p_states, s_states = [], []
    for l in range(DEPTH):
        lw = _layer_weights(l, *weights)
        xp, st = _prompt_layer(xp, pos_p, mem_prompt, lw, consts)
        p_states.append(st)
        xs, st = _sample_layer(l, xs, pos_s, caches, page_table, lw, consts)
        s_states.append(st)
    stack = lambda states, i: jnp.stack([st[i] for st in states])
    return ((xp, xs) + tuple(stack(p_states, i) for i in range(8)) + tuple(stack(s_states, i) for i in range(6)))
```

```python
import functools

import numpy as np
import jax
import jax.numpy as jnp
from jax import lax
from jax.experimental import pallas as pl
from jax.experimental.pallas import tpu as pltpu

F32 = jnp.float32
BF16 = jnp.bfloat16

DEPTH = 2
PAGE = 128
H_MLA = 8
NOPE = 64
ROPE = 32
V_DIM = 64
Q_RANK = 256
KV_RANK = 128
MLA_SCALE = (NOPE + ROPE) ** -0.5
ROPE_BASE = 10000.0
H_FOX = 8
D_FOX = 64
FOX_W = H_FOX * D_FOX
FOX_SCALE = D_FOX ** -0.5
MEM_TOK = 256
H_MEM = 4
D_MEM = 256
MEM_SCALE = D_MEM ** -0.5
LN_EPS = 1e-5
RMS_EPS = 1e-6
ALPHA = (2 * DEPTH) ** 0.25

LANES = 128
SUBLANES = 8
VMEM_LIMIT = 56 * 1024 * 1024

C_Q, C_KV, C_KR, C_KS, C_FQ, C_FK, C_FV, C_FL, C_END = 0, 256, 384, 512, 640, 1152, 1664, 2176, 2304

NEG_INF = float("-inf")


def _mm(a, b):
    return jnp.dot(a, b, preferred_element_type=F32)


def _mm_nt(a, b):
    return lax.dot_general(a, b, (((1,), (1,)), ((), ())), preferred_element_type=F32)


def _mm_tn(a, b):
    return lax.dot_general(a, b, (((0,), (0,)), ((), ())), preferred_element_type=F32)


def _rms(x, g):
    return x * lax.rsqrt(jnp.mean(x * x, axis=-1, keepdims=True) + RMS_EPS) * g


def _ln(x, g, b):
    mu = jnp.mean(x, axis=-1, keepdims=True)
    xc = x - mu
    var = jnp.mean(xc * xc, axis=-1, keepdims=True)
    return xc * lax.rsqrt(var + LN_EPS) * g + b


def _split3(x):
    hi = x.astype(BF16)
    r = x - hi.astype(F32)
    mid = r.astype(BF16)
    lo = (r - mid.astype(F32)).astype(BF16)
    return hi, mid, lo


def _const_spec(shape):
    nd = len(shape)
    return pl.BlockSpec(shape, lambda *_: (0,) * nd)


def _params(sem):
    return pltpu.CompilerParams(dimension_semantics=sem, vmem_limit_bytes=VMEM_LIMIT)


def _inproj_kernel(x_ref, pos_ref, w_ref, bf_ref, gq_ref, gkv_ref, wuq_ref, wuk_ref, inv_ref, sgn_ref,
                   qcat_ref, kcat_ref, vcat_ref, ckv_ref, kr_ref, fq_ref, fk_ref, fv_ref, lf_ref):
    xb = x_ref[0].astype(BF16)
    h = _mm(xb, w_ref[...])
    ang = pos_ref[0] * inv_ref[...]
    cos = jnp.cos(ang)
    sin = jnp.sin(ang) * sgn_ref[...]
    qn = _rms(h[:, C_Q:C_KV], gq_ref[...]).astype(BF16)
    qu = _mm(qn, wuq_ref[...])
    ckv = _rms(h[:, C_KV:C_KR], gkv_ref[...])
    kr = h[:, C_KR:C_KS] * cos + h[:, C_KS:C_FQ] * sin
    ckv_ref[0] = ckv
    kr_ref[0] = kr[:, :ROPE]
    kcat_ref[0] = jnp.concatenate([ckv.astype(BF16), kr.astype(BF16)], axis=1)
    vcat_ref[0] = jnp.concatenate([ckv.astype(BF16), jnp.ones(ckv.shape, BF16)], axis=1)
    lane =lax.broadcasted_iota(jnp.int32, (1, LANES), 1)
    slabs = []
    for g in range(2):
        lo = 512 + g * LANES
        slabs.append(qu[:, lo:lo + LANES] * cos + qu[:, lo + 256:lo + 256 + LANES] * sin)
    for hd in range(H_MLA):
        g2 = hd // 2
        qabs = _mm(qu[:, g2 * LANES:(g2 + 1) * LANES].astype(BF16), wuk_ref[hd])
        slab = jnp.where(lane // ROPE == hd % 4, slabs[hd // 4], 0.0)
        qcat_ref[0, hd] = jnp.concatenate(
            [(qabs * MLA_SCALE).astype(BF16), (slab * MLA_SCALE).astype(BF16)], axis=1)
    fq_ref[0] = (h[:, C_FQ:C_FK] * FOX_SCALE).astype(BF16)
    fk_ref[0] = h[:, C_FK:C_FV]
    fv_ref[0] = h[:, C_FV:C_FL]
    z = h[:, C_FL:C_END] + bf_ref[...]
    logf = jnp.minimum(z, 0.0) - jnp.log1p(jnp.exp(-jnp.abs(z)))
    lf_ref[0] = logf[:, :H_FOX]


def _inproj(x, pos, lw, tm):
    b, t, d = x.shape
    grid = (b, t // tm)
    tok = lambda w: pl.BlockSpec((1, tm, w), lambda i, j: (i, j, 0))
    out_shape = (
        jax.ShapeDtypeStruct((b, H_MLA, t, 256), BF16),
        jax.ShapeDtypeStruct((b, t, 256), BF16),
        jax.ShapeDtypeStruct((b, t, 256), BF16),
        jax.ShapeDtypeStruct((b, t, KV_RANK), F32),
        jax.ShapeDtypeStruct((b, t, ROPE), F32),
        jax.ShapeDtypeStruct((b, t, FOX_W), BF16),
        jax.ShapeDtypeStruct((b, t, FOX_W), F32),
        jax.ShapeDtypeStruct((b, t, FOX_W), F32),
        jax.ShapeDtypeStruct((b, t, H_FOX), F32),
    )
    out_specs = (
        pl.BlockSpec((1, H_MLA, tm, 256), lambda i, j: (i, 0, j, 0)),
        tok(256), tok(256), tok(KV_RANK), tok(ROPE), tok(FOX_W), tok(FOX_W), tok(FOX_W), tok(H_FOX),
    )
    in_specs = [
        tok(d), tok(1),
        _const_spec(lw["w_in"].shape), _const_spec((1, LANES)), _const_spec((1, Q_RANK)),
        _const_spec((1, KV_RANK)), _const_spec(lw["w_uq"].shape), _const_spec(lw["w_uk"].shape),
        _const_spec((1, LANES)), _const_spec((1, LANES)),
    ]
    return pl.pallas_call(
        _inproj_kernel, grid=grid, in_specs=in_specs, out_specs=out_specs, out_shape=out_shape,
        compiler_params=_params(("parallel", "parallel")), name="inproj",
    )(x, pos, lw["w_in"], lw["b_forget"], lw["g_q"], lw["g_kv"], lw["w_uq"], lw["w_uk"],
      lw["inv_freq"], lw["rope_sign"])


def _foxprep_kernel(fq_ref, fk_ref, fv_ref, lf_ref, tri_ref, pq_ref, pk_ref, oq_ref, ok_ref,
                    qcat_ref, kcat_ref, vb_ref, carry_ref):
    tm = lf_ref.shape[1]

    @pl.when(pl.program_id(1) == 0)
    def _():
        carry_ref[...] = jnp.zeros_like(carry_ref)

    tri = tri_ref[...]
    c = carry_ref[...]
    for part in _split3(lf_ref[0]):
        c = c + _mm(tri, part)
    carry_ref[...] = c[tm - 1:tm, :]
    parts = _split3(c)
    augq = oq_ref[...]
    augk = ok_ref[...]
    for j in range(3):
        augq = augq + _mm(parts[j], pq_ref[j])
        augk = augk + _mm(parts[j], pk_ref[j])
    augk = augk.astype(BF16)
    lane = lax.broadcasted_iota(jnp.int32, (1, LANES), 1)
    fq = fq_ref[0]
    fkb = fk_ref[0].astype(BF16)
    fvb = fv_ref[0].astype(BF16)
    for g in range(H_FOX // 2):
        kcat_ref[0, g] = jnp.concatenate([fkb[:, g * LANES:(g + 1) * LANES], augk], axis=1)
        vb_ref[0, g] = jnp.concatenate([fvb[:, g * LANES:(g + 1) * LANES], jnp.ones((tm, LANES), BF16)], axis=1)
    for hd in range(H_FOX):
        g = hd // 2
        qs = jnp.where(lane // D_FOX == hd % 2, fq[:, g * LANES:(g + 1) * LANES].astype(F32), 0.0)
        aq = jnp.where(lane // 8 == hd, augq, 0.0)
        qcat_ref[0, hd] = jnp.concatenate([qs.astype(BF16), aq.astype(BF16)], axis=1)


def _foxprep(fq, fk, fv, lf, consts, tm):
    b, t, _ = fq.shape
    tok = lambda w: pl.BlockSpec((1, tm, w), lambda i, j: (i, j, 0))
    out_shape = (
        jax.ShapeDtypeStruct((b, H_FOX, t, 256), BF16),
        jax.ShapeDtypeStruct((b, H_FOX // 2, t, 256), BF16),
        jax.ShapeDtypeStruct((b, H_FOX // 2, t, 256), BF16),
    )
    out_specs = (
        pl.BlockSpec((1, H_FOX, tm, 256), lambda i, j: (i, 0, j, 0)),
        pl.BlockSpec((1, H_FOX // 2, tm, 256), lambda i, j: (i, 0, j, 0)),
        pl.BlockSpec((1, H_FOX // 2, tm, 256), lambda i, j: (i, 0, j, 0)),
    )
    in_specs = [tok(FOX_W), tok(FOX_W), tok(FOX_W), tok(H_FOX),
                _const_spec((tm, tm)), _const_spec((3, H_FOX, LANES)), _const_spec((3, H_FOX, LANES)),
                _const_spec((1, LANES)), _const_spec((1, LANES))]
    return pl.pallas_call(
        _foxprep_kernel, grid=(b, t // tm), in_specs=in_specs, out_specs=out_specs, out_shape=out_shape,
        scratch_shapes=[pltpu.VMEM((1, H_FOX), F32)],
        compiler_params=_params(("parallel", "arbitrary")), name="foxprep",
    )(fq, fk, fv, lf, consts["tri"], consts["place_q"], consts["place_k"], consts["ones_q"], consts["ones_k"])


def _tile_pairs(t, tq, tk):
    qi, kj = [], []
    for i in range(t // tq):
        for j in range(((i + 1) * tq - 1) // tk + 1):
            qi.append(i)
            kj.append(j)
    return jnp.asarray(np.array(qi, np.int32)), jnp.asarray(np.array(kj, np.int32))


def _flash_step(s, v_ones, m_ref, acc_ref, idx=Ellipsis):
    m_old = m_ref[idx]
    m_new = jnp.maximum(m_old, jnp.max(s, axis=1, keepdims=True))
    alpha = jnp.exp(m_old - m_new)
    pr = jnp.exp(s - jnp.concatenate([m_new] * (s.shape[1] // LANES), axis=1))
    acc_ref[idx] = jnp.concatenate([alpha, alpha], axis=1) * acc_ref[idx] + _mm(pr.astype(BF16), v_ones)
    m_ref[idx] = m_new


def _mla_flash_kernel(qi_ref, kj_ref, q_ref, k_ref, v_ref, o_ref, m_ref, acc_ref, *, tq, tk):
    p = pl.program_id(1)
    qi = qi_ref[p]
    kj = kj_ref[p]
    rows = H_MLA * tq

    @pl.when(kj == 0)
    def _():
        m_ref[...] = jnp.full_like(m_ref, NEG_INF)
        acc_ref[...] = jnp.zeros_like(acc_ref)

    def update(masked):
        s = _mm_nt(q_ref[0].reshape(rows, 256), k_ref[0])
        if masked:
            row = qi * tq + lax.broadcasted_iota(jnp.int32, (rows, tk), 0) % tq
            col = kj * tk + lax.broadcasted_iota(jnp.int32, (rows, tk), 1)
            s = jnp.where(col <= row, s, NEG_INF)
        _flash_step(s, v_ref[0], m_ref, acc_ref)

    straddle = kj * tk + tk - 1 > qi * tq
    pl.when(straddle)(functools.partial(update, True))
    pl.when(jnp.logical_not(straddle))(functools.partial(update, False))

    @pl.when((kj + 1) * tk >= (qi + 1) * tq)
    def _():
        acc = acc_ref[...]
        o = acc[:, :KV_RANK] / acc[:, KV_RANK:]
        for hd in range(H_MLA):
            o_ref[0, :, hd * KV_RANK:(hd + 1) * KV_RANK] = o[hd * tq:(hd + 1) * tq]


def _mla_flash(qcat, kcat, vcat, tq, tk):
    b, _, t, _ = qcat.shape
    qi, kj = _tile_pairs(t, tq, tk)
    kv_spec = pl.BlockSpec((1, tk, 256), lambda i, p, qi, kj: (i, kj[p], 0))
    grid_spec = pltpu.PrefetchScalarGridSpec(
        num_scalar_prefetch=2, grid=(b, int(qi.shape[0])),
        in_specs=[pl.BlockSpec((1, H_MLA, tq, 256), lambda i, p, qi, kj: (i, 0, qi[p], 0)), kv_spec, kv_spec],
        out_specs=pl.BlockSpec((1, tq, H_MLA * KV_RANK), lambda i, p, qi, kj: (i, qi[p], 0)),
        scratch_shapes=[pltpu.VMEM((H_MLA * tq, LANES), F32), pltpu.VMEM((H_MLA * tq, 2 * KV_RANK), F32)])
    return pl.pallas_call(
        functools.partial(_mla_flash_kernel, tq=tq, tk=tk), grid_spec=grid_spec,
        out_shape=jax.ShapeDtypeStruct((b, t, H_MLA * KV_RANK), F32),
        compiler_params=_params(("parallel", "arbitrary")), name="mla_flash",
    )(qi, kj, qcat, kcat, vcat)


def _fox_flash_kernel(qi_ref, kj_ref, q_ref, k_ref, v_ref, o_ref, m_ref, acc_ref, *, t):
    p = pl.program_id(1)
    qi = qi_ref[p]
    kj = kj_ref[p]

    @pl.when(kj == 0)
    def _():
        m_ref[...] = jnp.full_like(m_ref, NEG_INF)
        acc_ref[...] = jnp.zeros_like(acc_ref)

    def heads(masked):
        def body(g, carry):
            q2 = q_ref[0, pl.ds(2 * g, 2)].reshape(2 * t, 256)
            s = _mm_nt(q2, k_ref[0, g])
            if masked:
                row = lax.broadcasted_iota(jnp.int32, (2 * t, t), 0) % t
                col = lax.broadcasted_iota(jnp.int32, (2 * t, t), 1)
                s = jnp.where(col <= row, s, NEG_INF)
            _flash_step(s, v_ref[0, g], m_ref, acc_ref, g)
            return carry
        lax.fori_loop(0, H_FOX // 2, body, 0, unroll=2)

    pl.when(kj == qi)(functools.partial(heads, True))
    pl.when(kj != qi)(functools.partial(heads, False))

    @pl.when(kj == qi)
    def _():
        lane = lax.broadcasted_iota(jnp.int32, (1, LANES), 1)
        for g in range(H_FOX // 2):
            oa = acc_ref[g, :t, :LANES] / acc_ref[g, :t, LANES:]
            ob = acc_ref[g, t:, :LANES] / acc_ref[g, t:, LANES:]
            o_ref[0, :, g * LANES:(g + 1) * LANES] = jnp.where(lane < D_FOX, oa, ob)


def _fox_flash(qcat, kcat, vb, t_blk):
    b, _, t, _ = qcat.shape
    qi, kj = _tile_pairs(t, t_blk, t_blk)
    grid_spec = pltpu.PrefetchScalarGridSpec(
        num_scalar_prefetch=2, grid=(b, int(qi.shape[0])),
        in_specs=[pl.BlockSpec((1, H_FOX, t_blk, 256), lambda i, p, qi, kj: (i, 0, qi[p], 0)),
                  pl.BlockSpec((1, H_FOX // 2, t_blk, 256), lambda i, p, qi, kj: (i, 0, kj[p], 0)),
                  pl.BlockSpec((1, H_FOX // 2, t_blk, 256), lambda i, p, qi, kj: (i, 0, kj[p], 0))],
        out_specs=pl.BlockSpec((1, t_blk, FOX_W), lambda i, p, qi, kj: (i, qi[p], 0)),
        scratch_shapes=[pltpu.VMEM((H_FOX // 2, 2 * t_blk, LANES), F32),
                        pltpu.VMEM((H_FOX // 2, 2 * t_blk, 2 * LANES), F32)])
    return pl.pallas_call(
        functools.partial(_fox_flash_kernel, t=t_blk), grid_spec=grid_spec,
        out_shape=jax.ShapeDtypeStruct((b, t, FOX_W), F32),
        compiler_params=_params(("parallel", "arbitrary")), name="fox_flash",
    )(qi, kj, qcat, kcat, vb)


def _softmax_step(s, m_ref, l_ref):
    m_old = m_ref[...]
    m_new = jnp.maximum(m_old, jnp.max(s, axis=1, keepdims=True))
    alpha = jnp.exp(m_old - m_new)
    pr = jnp.exp(s - m_new)
    l_ref[...] = alpha * l_ref[...] + jnp.sum(pr, axis=1, keepdims=True)
    m_ref[...] = m_new
    return pr, alpha


def _mla_decode_phases(ckv_pages, krt_pages, qa_ref, qr_ref, ckn_ref, krn_ref, o_ref, m_ref, l_ref, acc_ref, *,
                       n_new):
    def init():
        m_ref[...] = jnp.full_like(m_ref, NEG_INF)
        l_ref[...] = jnp.zeros_like(l_ref)
        acc_ref[...] = jnp.zeros_like(acc_ref)

    def update(ckv, krt, mask):
        qa = qa_ref[0]
        s = _mm_nt(qa, ckv) + _mm(qr_ref[0], krt)
        if mask is not None:
            s = jnp.where(mask, s, NEG_INF)
        pr, alpha = _softmax_step(s, m_ref, l_ref)
        acc_ref[...] = alpha * acc_ref[...] + _mm(pr.astype(BF16), ckv)

    def pages():
        ckv = jnp.concatenate([r[...].astype(BF16) for r in ckv_pages], axis=0)
        krt = jnp.concatenate([r[...].astype(BF16) for r in krt_pages], axis=1)
        update(ckv, krt, None)

    def finish():
        rows = qa_ref.shape[1]
        key = lax.broadcasted_iota(jnp.int32, (rows, PAGE), 1)
        qry = lax.broadcasted_iota(jnp.int32, (rows, PAGE), 0) % n_new
        update(ckn_ref[0].astype(BF16), krn_ref[0].astype(BF16), key <= qry)
        o_ref[0] = acc_ref[...] / l_ref[...]

    return init, pages, finish


def _fox_decode_phases(kt_pages, vt_pages, lf_pages, qbd_ref, triu_ref, knt_ref, vnt_ref, lfn_ref, o_ref,
                       m_ref, l_ref, acc_ref, carry_ref, *, n_new):
    rows = n_new * H_FOX

    def init():
        m_ref[...] = jnp.full_like(m_ref, NEG_INF)
        l_ref[...] = jnp.zeros_like(l_ref)
        acc_ref[...] = jnp.zeros_like(acc_ref)
        carry_ref[...] = jnp.zeros_like(carry_ref)

    def col_to_row(x):
        return jnp.transpose(jnp.broadcast_to(x, (rows, LANES)))[0:1, :]

    def update(kt, vt, lf_rows, mask):
        qbd = qbd_ref[0]
        triu = triu_ref[...]
        n = lf_rows.shape[0] // H_FOX
        cw = jnp.zeros(lf_rows.shape, F32)
        for part in _split3(lf_rows):
            cw = cw + _mm(part, triu)
        tot = jnp.sum(lf_rows, axis=1, keepdims=True)
        carry = carry_ref[...]
        bias = []
        for j in range(n):
            bias.append(cw[j * H_FOX:(j + 1) * H_FOX] + carry)
            carry = carry + tot[j * H_FOX:(j + 1) * H_FOX]
        carry_ref[...] = carry
        b = jnp.concatenate(bias, axis=1) if n > 1 else bias[0]
        s = _mm(qbd, kt)
        s = (s.reshape(n_new, H_FOX, s.shape[1]) - b[None]).reshape(rows, s.shape[1])
        if mask is not None:
            s = jnp.where(mask, s, NEG_INF)
        pr, alpha = _softmax_step(s, m_ref, l_ref)
        acc_ref[...] = col_to_row(alpha) * acc_ref[...] + _mm_nt(vt, pr.astype(BF16))

    def pages():
        update(jnp.concatenate([r[...].astype(BF16) for r in kt_pages], axis=1),
               jnp.concatenate([r[...].astype(BF16) for r in vt_pages], axis=1),
               jnp.concatenate([r[...] for r in lf_pages], axis=0), None)

    def finish():
        key = lax.broadcasted_iota(jnp.int32, (rows, PAGE), 1)
        qry = lax.broadcasted_iota(jnp.int32, (rows, PAGE), 0) // H_FOX
        update(knt_ref[0].astype(BF16), vnt_ref[0].astype(BF16), lfn_ref[0], key <= qry)
        lane = lax.broadcasted_iota(jnp.int32, (1, rows), 1) % H_FOX
        o = jnp.zeros((D_FOX, rows), F32)
        for hd in range(H_FOX):
            o = o + jnp.where(lane == hd, acc_ref[hd * D_FOX:(hd + 1) * D_FOX, :], 0.0)
        o_ref[0] = o / col_to_row(l_ref[...])

    return init, pages, finish


N_DECODE_STREAMS = 5
N_DECODE_SEQ_INPUTS = 9
N_DECODE_STATE = 7


def _decode_phases(step, last, pt_ref, hbms, seq_refs, out_refs, bufs, sem, state, *, layer, pps, nchunk, n_new):
    (qbd_ref, triu_ref, knt_ref, vnt_ref, lfn_ref, qa_ref, qr_ref, ckn_ref, krn_ref) = seq_refs
    of_ref, ol_ref = out_refs
    fm_ref, fl_ref, facc_ref, carry_ref, mm_ref, ml_ref, macc_ref = state
    n_slots = bufs[0].shape[0]
    ahead = n_slots - 1
    c = step % nchunk
    slot = step % n_slots
    streams = tuple(zip(hbms, bufs))

    def copies(v):
        st = jnp.minimum(v, last)
        sl = v % n_slots
        out = []
        for j in range(pps):
            page = pt_ref[st * pps + j]
            for k, (hbm, buf) in enumerate(streams):
                out.append(pltpu.make_async_copy(hbm.at[layer, page], buf.at[sl, j], sem.at[sl, k]))
        return out

    views = [[buf.at[slot, j] for j in range(pps)] for _, buf in streams]
    mla = _mla_decode_phases(views[3], views[4], qa_ref, qr_ref, ckn_ref, krn_ref, ol_ref, mm_ref, ml_ref,
                             macc_ref, n_new=n_new)
    fox = _fox_decode_phases(views[0], views[1], views[2], qbd_ref, triu_ref, knt_ref, vnt_ref, lfn_ref, of_ref,
                             fm_ref, fl_ref, facc_ref, carry_ref, n_new=n_new)

    def prologue():
        for v in range(ahead):
            for cp in copies(v):
                cp.start()

    def fetch():
        for cp in copies(step):
            cp.wait()
        for cp in copies(step + ahead):
            cp.start()

    def epilogue():
        for v in range(1, ahead + 1):
            for cp in copies(step + v):
                cp.wait()

    def both(i):
        def run():
            mla[i]()
            fox[i]()
        return run

    return c, prologue, fetch, both(0), both(1), both(2), epilogue


def _decode_kernel(pt_ref, *refs, layer, pps, nchunk, n_new):
    ns, nq = N_DECODE_STREAMS, N_DECODE_SEQ_INPUTS
    step = pl.program_id(0) * nchunk + pl.program_id(1)
    last = pl.num_programs(0) * nchunk - 1
    c, prologue, fetch, init, pages, finish, epilogue = _decode_phases(
        step, last, pt_ref, refs[:ns], refs[ns:ns + nq], refs[ns + nq:ns + nq + 2],
        refs[ns + nq + 2:2 * ns + nq + 2], refs[2 * ns + nq + 2], refs[2 * ns + nq + 3:],
        layer=layer, pps=pps, nchunk=nchunk, n_new=n_new)
    pl.when(step == 0)(prologue)
    pl.when(c == 0)(init)
    fetch()
    pages()
    pl.when(c == nchunk - 1)(finish)
    pl.when(step == last)(epilogue)


def _decode_operands(page_table, pps, seq_index):
    n_new = 8
    rows = n_new * H_FOX
    page_shapes = ((FOX_W, PAGE), (FOX_W, PAGE), (H_FOX, PAGE), (PAGE, KV_RANK), (ROPE, PAGE))
    hbm = pl.BlockSpec(memory_space=pl.ANY)
    per_seq = lambda shape: pl.BlockSpec((1,) + shape, lambda *a: (seq_index(*a),) + (0,) * len(shape))
    const = lambda shape: pl.BlockSpec(shape, lambda *a: (0,) * len(shape))
    in_specs = ([hbm] * len(page_shapes)
                + [per_seq((rows, FOX_W)), const((PAGE, PAGE)),
                   per_seq((FOX_W, PAGE)), per_seq((FOX_W, PAGE)), per_seq((H_FOX, PAGE)),
                   per_seq((rows, KV_RANK)), per_seq((rows, ROPE)), per_seq((PAGE, KV_RANK)),
                   per_seq((ROPE, PAGE))])
    out_specs = [per_seq((D_FOX, rows)), per_seq((rows, KV_RANK))]
    bs = page_table.shape[0]
    out_shape = [jax.ShapeDtypeStruct((bs, D_FOX, rows), F32), jax.ShapeDtypeStruct((bs, rows, KV_RANK), F32)]
    n_slots = 3
    scratch = ([pltpu.VMEM((n_slots, pps) + shape, F32) for shape in page_shapes]
               + [pltpu.SemaphoreType.DMA((n_slots, len(page_shapes))),
                  pltpu.VMEM((rows, 1), F32), pltpu.VMEM((rows, 1), F32),
                  pltpu.VMEM((FOX_W, rows), F32), pltpu.VMEM((H_FOX, 1), F32),
                  pltpu.VMEM((rows, 1), F32), pltpu.VMEM((rows, 1), F32), pltpu.VMEM((rows, KV_RANK), F32)])
    return in_specs, out_specs, out_shape, scratch, n_new


def _decode_args(pools, fox_new, mla_new, consts):
    qbd, knt, vnt, lfnt = fox_new
    qa, qr, ckn, krnt = mla_new
    return tuple(pools) + (qbd, consts["triu_page"], knt, vnt, lfnt, qa, qr, ckn, krnt)


def _decode(layer, page_table, pools, fox_new, mla_new, consts, pps):
    bs, npages = page_table.shape
    nchunk = npages // pps
    assert npages == nchunk * pps
    in_specs, out_specs, out_shape, scratch, n_new = _decode_operands(page_table, pps, lambda b, c, pt: b)
    grid_spec = pltpu.PrefetchScalarGridSpec(
        num_scalar_prefetch=1, grid=(bs, nchunk), in_specs=in_specs, out_specs=tuple(out_specs),
        scratch_shapes=scratch)
    return pl.pallas_call(
        functools.partial(_decode_kernel, layer=layer, pps=pps, nchunk=nchunk, n_new=n_new), grid_spec=grid_spec,
        out_shape=tuple(out_shape), compiler_params=_params(("arbitrary", "arbitrary")), name="decode",
    )(page_table.reshape(-1), *_decode_args(pools, fox_new, mla_new, consts))


def _mla_flash_decode_kernel(qi_ref, kj_ref, pt_ref, q_ref, k_ref, v_ref, *refs, tq, tk, layer, pps, nchunk, n_dec,
                             n_new):
    ns, nq = N_DECODE_STREAMS, N_DECODE_SEQ_INPUTS
    hbms, seq_refs = refs[:ns], refs[ns:ns + nq]
    o_ref = refs[ns + nq]
    dec_out = refs[ns + nq + 1:ns + nq + 3]
    m_ref, acc_ref = refs[ns + nq + 3:ns + nq + 5]
    bufs = refs[ns + nq + 5:2 * ns + nq + 5]
    sem = refs[2 * ns + nq + 5]
    state = refs[2 * ns + nq + 6:]
    p = pl.program_id(1)
    f = pl.program_id(0) * pl.num_programs(1) + p
    qi = qi_ref[p]
    kj = kj_ref[p]
    rows = H_MLA * tq

    @pl.when(kj == 0)
    def _():
        m_ref[...] = jnp.full_like(m_ref, NEG_INF)
        acc_ref[...] = jnp.zeros_like(acc_ref)

    def flash(masked):
        s = _mm_nt(q_ref[0].reshape(rows, 256), k_ref[0])
        if masked:
            row = qi * tq + lax.broadcasted_iota(jnp.int32, (rows, tk), 0) % tq
            col = kj * tk + lax.broadcasted_iota(jnp.int32, (rows, tk), 1)
            s = jnp.where(col <= row, s, NEG_INF)
        _flash_step(s, v_ref[0], m_ref, acc_ref)

    step = jnp.minimum(f, n_dec - 1)
    c, prologue, fetch, init, pages, finish, epilogue = _decode_phases(
        step, n_dec - 1, pt_ref, hbms, seq_refs, dec_out, bufs, sem, state,
        layer=layer, pps=pps, nchunk=nchunk, n_new=n_new)
    has_dec = f < n_dec
    straddle = kj * tk + tk - 1 > qi * tq
    pl.when(f == 0)(prologue)
    pl.when(jnp.logical_and(has_dec, c == 0))(init)
    for with_dec in (True, False):
        for masked in (True, False):
            def body(with_dec=with_dec, masked=masked):
                if with_dec:
                    fetch()
                    pages()
                flash(masked)
            pl.when(jnp.logical_and(has_dec == with_dec, straddle == masked))(body)
    pl.when(jnp.logical_and(has_dec, c == nchunk - 1))(finish)
    pl.when(f == n_dec - 1)(epilogue)

    @pl.when((kj + 1) * tk >= (qi + 1) * tq)
    def _():
        acc = acc_ref[...]
        o = acc[:, :KV_RANK] / acc[:, KV_RANK:]
        for hd in range(H_MLA):
            o_ref[0, :, hd * KV_RANK:(hd + 1) * KV_RANK] = o[hd * tq:(hd + 1) * tq]


def _mla_flash_decode(layer, qcat, kcat, vcat, tq, tk, page_table, pools, fox_new, mla_new, consts, pps):
    b, _, t, _ = qcat.shape
    qi, kj = _tile_pairs(t, tq, tk)
    npairs = int(qi.shape[0])
    bs, npages = page_table.shape
    nchunk = npages // pps
    n_dec = bs * nchunk
    assert npages == nchunk * pps and b * npairs >= n_dec
    seq = lambda i, p, qi, kj, pt: jnp.minimum(i * npairs + p, n_dec - 1) // nchunk
    d_in, d_out, d_shape, d_scratch, n_new = _decode_operands(page_table, pps, seq)
    kv_spec = pl.BlockSpec((1, tk, 256), lambda i, p, qi, kj, pt: (i, kj[p], 0))
    grid_spec = pltpu.PrefetchScalarGridSpec(
        num_scalar_prefetch=3, grid=(b, npairs),
        in_specs=[pl.BlockSpec((1, H_MLA, tq, 256), lambda i, p, qi, kj, pt: (i, 0, qi[p], 0)), kv_spec,
                  kv_spec] + d_in,
        out_specs=tuple([pl.BlockSpec((1, tq, H_MLA * KV_RANK), lambda i, p, qi, kj, pt: (i, qi[p], 0))] + d_out),
        scratch_shapes=[pltpu.VMEM((H_MLA * tq, LANES), F32), pltpu.VMEM((H_MLA * tq, 2 * KV_RANK), F32)]
        + d_scratch)
    return pl.pallas_call(
        functools.partial(_mla_flash_decode_kernel, tq=tq, tk=tk, layer=layer, pps=pps, nchunk=nchunk,
                          n_dec=n_dec, n_new=n_new),
        grid_spec=grid_spec, out_shape=tuple([jax.ShapeDtypeStruct((b, t, H_MLA * KV_RANK), F32)] + d_shape),
        compiler_params=_params(("arbitrary", "arbitrary")), name="mla_flash_decode",
    )(qi, kj, page_table.reshape(-1), qcat, kcat, vcat, *_decode_args(pools, fox_new, mla_new, consts))


def _outproj_kernel(x_ref, ol_ref, of_ref, wuv_ref, gm_ref, gf_ref, wo_ref, g1_ref, b1_ref, wmq_ref,
                    x1_ref, mq_ref):
    o_mla = _mm(ol_ref[0].astype(BF16), wuv_ref[...])
    nm = _rms(o_mla, gm_ref[...]).astype(BF16)
    nf = _rms(of_ref[0], gf_ref[...]).astype(BF16)
    mix = _mm(jnp.concatenate([nm, nf], axis=1), wo_ref[...])
    x1 = _ln(ALPHA * x_ref[0] + mix, g1_ref[...], b1_ref[...])
    x1_ref[0] = x1
    mq_ref[0] = _mm(x1.astype(BF16), wmq_ref[...]) * MEM_SCALE


def _outproj(x, o_lat, o_fox, lw, tm):
    b, t, d = x.shape
    tok = lambda w: pl.BlockSpec((1, tm, w), lambda i, j: (i, j, 0))
    in_specs = [tok(d), tok(H_MLA * KV_RANK), tok(FOX_W),
                _const_spec(lw["w_uv"].shape), _const_spec((1, FOX_W)), _const_spec((1, FOX_W)),
                _const_spec(lw["w_o"].shape), _const_spec((1, d)), _const_spec((1, d)),
                _const_spec(lw["w_mq"].shape)]
    return pl.pallas_call(
        _outproj_kernel, grid=(b, t // tm), in_specs=in_specs, out_specs=(tok(d), tok(d)),
        out_shape=(jax.ShapeDtypeStruct((b, t, d), F32), jax.ShapeDtypeStruct((b, t, d), F32)),
        compiler_params=_params(("parallel", "parallel")), name="outproj",
    )(x, o_lat, o_fox, lw["w_uv"], lw["g_mla_out"], lw["g_fox_out"], lw["w_o"], lw["ln1_g"], lw["ln1_b"],
      lw["w_mq"])


def _memkv_kernel(m_ref, wk_ref, wv_ref, k_ref, v_ref):
    mb = m_ref[...].astype(BF16)
    k_ref[...] = _mm(mb, wk_ref[...])
    v_ref[...] = _mm(mb, wv_ref[...])


def _memkv(mem, wk, wv):
    n, d = mem.shape
    return pl.pallas_call(
        _memkv_kernel, grid=(1,),
        in_specs=[_const_spec((n, d)), _const_spec(wk.shape), _const_spec(wv.shape)],
        out_specs=(_const_spec((n, d)), _const_spec((n, d))),
        out_shape=(jax.ShapeDtypeStruct((n, d), F32), jax.ShapeDtypeStruct((n, d), F32)),
        compiler_params=_params(("arbitrary",)), name="memkv",
    )(mem, wk, wv)


def _cross_kernel(mq_ref, mk_ref, mv_ref, o_ref):
    mq = mq_ref[0].astype(BF16)
    outs = []
    for hd in range(H_MEM):
        mk = mk_ref[0][:, hd * D_MEM:(hd + 1) * D_MEM]
        mv = mv_ref[0][:, hd * D_MEM:(hd + 1) * D_MEM]
        s = _mm_nt(mq[:, hd * D_MEM:(hd + 1) * D_MEM], mk.astype(BF16))
        pr = jnp.exp(s - jnp.max(s, axis=1, keepdims=True))
        pr = pr / jnp.sum(pr, axis=1, keepdims=True)
        outs.append(_mm(pr.astype(BF16), mv.astype(BF16)))
    o_ref[0] = jnp.concatenate(outs, axis=1)


def _cross(mq, mk, mv, tm):
    b, t, d = mq.shape
    tok = pl.BlockSpec((1, tm, d), lambda i, j: (i, j, 0))
    mem = pl.BlockSpec((1, MEM_TOK, d), lambda i, j: (i, 0, 0))
    return pl.pallas_call(
        _cross_kernel, grid=(b, t // tm), in_specs=[tok, mem, mem], out_specs=tok,
        out_shape=jax.ShapeDtypeStruct((b, t, d), F32),
        compiler_params=_params(("parallel", "parallel")), name="cross",
    )(mq, mk, mv)


def _cross_seq_kernel(mq_ref, mk_ref, mv_ref, o_ref):
    n_q = mq_ref.shape[1]
    cols = H_MEM * n_q
    mq = mq_ref[0]
    qs = jnp.concatenate([mq[:, hd * D_MEM:(hd + 1) * D_MEM] for hd in range(H_MEM)], axis=0).astype(BF16)
    kf = mk_ref[0, 0].reshape(MEM_TOK * H_MEM, D_MEM).astype(BF16)
    vf = mv_ref[0, 0].reshape(MEM_TOK * H_MEM, D_MEM).astype(BF16)
    s3 = _mm_nt(kf, qs).reshape(MEM_TOK * H_MEM // SUBLANES, SUBLANES, cols)
    sub = lax.broadcasted_iota(jnp.int32, (SUBLANES, cols), 0)
    lane = lax.broadcasted_iota(jnp.int32, (SUBLANES, cols), 1)
    diag = sub % H_MEM == lane // n_q
    mx = jnp.max(s3, axis=0)
    mx = jnp.maximum(mx, pltpu.roll(mx, H_MEM, 0))
    ex = jnp.where(diag[None], jnp.exp(s3 - mx[None]), 0.0)
    den = jnp.sum(ex, axis=0)
    den = den + pltpu.roll(den, H_MEM, 0)
    pr = jnp.where(diag[None], ex / den[None], 0.0).reshape(MEM_TOK * H_MEM, cols)
    o = _mm_tn(pr.astype(BF16), vf)
    o_ref[0] = jnp.concatenate([o[hd * n_q:(hd + 1) * n_q] for hd in range(H_MEM)], axis=1)


def _cross_seq(mq, mk, mv, layer):
    b, t, d = mq.shape
    tok = pl.BlockSpec((1, t, d), lambda i: (i, 0, 0))
    mem = pl.BlockSpec((1, 1, MEM_TOK, H_MEM, D_MEM), lambda i: (layer, i, 0, 0, 0))
    return pl.pallas_call(
        _cross_seq_kernel, grid=(b,), in_specs=[tok, mem, mem], out_specs=tok,
        out_shape=jax.ShapeDtypeStruct((b, t, d), F32),
        compiler_params=_params(("parallel",)), name="cross_seq",
    )(mq, mk, mv)


def _ffn_kernel(*refs, seq_mode, n_chunks):
    if seq_mode:
        (x_ref, mo_ref, wmo_ref, g2_ref, b2_ref, wup_ref, cw_ref, cb_ref, wdn_ref, g3_ref, b3_ref,
         y_ref, u_ref, carry_ref) = refs
    else:
        (x_ref, mo_ref, pre_ref, wmo_ref, g2_ref, b2_ref, wup_ref, cw_ref, cb_ref, wdn_ref, g3_ref, b3_ref,
         y_ref, u_ref) = refs
    tm = x_ref.shape[1]
    dff = wdn_ref.shape[0]
    fc = dff // n_chunks

    if seq_mode:
        @pl.when(pl.program_id(1) == 0)
        def _():
            carry_ref[...] = jnp.zeros_like(carry_ref)

    x2 = _ln(ALPHA * x_ref[0] + _mm(mo_ref[0].astype(BF16), wmo_ref[...]), g2_ref[...], b2_ref[...])
    xb = x2.astype(BF16)
    row = lax.broadcasted_iota(jnp.int32, (tm, 1), 0)

    def conv(lo):
        u = _mm(xb, wup_ref[:, lo:lo + fc])
        if seq_mode:
            ext = jnp.concatenate([carry_ref[:, lo:lo + fc], u], axis=0)
            u1 = pltpu.roll(ext, 1, 0)[SUBLANES:]
            u2 = pltpu.roll(ext, 2, 0)[SUBLANES:]
            carry_ref[:, lo:lo + fc] = u[tm - SUBLANES:]
            u_ref[0, 0, :, lo:lo + fc] = u[tm - SUBLANES:]
        else:
            e = pre_ref[0, :, lo:lo + fc]
            u1 = jnp.where(row % SUBLANES < 1, pltpu.roll(e, tm - 1, 0), pltpu.roll(u, 1, 0))
            u2 = jnp.where(row % SUBLANES < 2, e, pltpu.roll(u, 2, 0))
            u_ref[0, :, lo:lo + fc] = u
        cw = cw_ref[:, lo:lo + fc]
        return cb_ref[:, lo:lo + fc] + (cw[0:1] * u2 + cw[1:2] * u1 + cw[2:3] * u)

    ffn = jnp.zeros((tm, x_ref.shape[2]), F32)
    for ci in range(n_chunks):
        val = conv(ci * fc)
        gate = conv(dff + ci * fc)
        hcat = (gate * jax.nn.sigmoid(gate) * val).astype(BF16)
        ffn = ffn + _mm(hcat, wdn_ref[ci * fc:(ci + 1) * fc, :])
    y_ref[0] = _ln(ALPHA * x2 + ffn, g3_ref[...], b3_ref[...])


def _ffn(x1, mo, lw, tm, pre=None):
    b, t, d = x1.shape
    f2 = lw["w_up"].shape[1]
    seq_mode = pre is None
    tok = lambda w: pl.BlockSpec((1, tm, w), lambda i, j: (i, j, 0))
    consts = [_const_spec(lw["w_mo"].shape), _const_spec((1, d)), _const_spec((1, d)),
              _const_spec(lw["w_up"].shape), _const_spec((3, f2)), _const_spec((1, f2)),
              _const_spec(lw["w_down"].shape), _const_spec((1, d)), _const_spec((1, d))]
    cargs = (lw["w_mo"], lw["ln2_g"], lw["ln2_b"], lw["w_up"], lw["conv_w"], lw["conv_b"], lw["w_down"],
             lw["ln3_g"], lw["ln3_b"])
    if seq_mode:
        in_specs = [tok(d), tok(d)] + consts
        args = (x1, mo) + cargs
        u_shape = jax.ShapeDtypeStruct((b, t // tm, SUBLANES, f2), F32)
        u_spec = pl.BlockSpec((1, 1, SUBLANES, f2), lambda i, j: (i, j, 0, 0))
        scratch = [pltpu.VMEM((SUBLANES, f2), F32)]
    else:
        in_specs = [tok(d), tok(d), tok(f2)] + consts
        args = (x1, mo, pre) + cargs
        u_shape = jax.ShapeDtypeStruct((b, t, f2), F32)
        u_spec = tok(f2)
        scratch = []
    return pl.pallas_call(
        functools.partial(_ffn_kernel, seq_mode=seq_mode, n_chunks=2), grid=(b, t // tm),
        in_specs=in_specs, out_specs=(tok(d), u_spec),
        out_shape=(jax.ShapeDtypeStruct((b, t, d), F32), u_shape), scratch_shapes=scratch,
        compiler_params=_params(("parallel", "arbitrary")), name="ffn",
    )(*args)


def _layer_weights(l, w_in, b_forget, g_q_lat, g_kv_lat, w_uq, w_uk, w_uv, g_mla_out, g_fox_out, w_o, ln1_g,
                   ln1_b, w_mq, w_mk, w_mv, w_mo, ln2_g, ln2_b, w_up, conv_w, conv_b, w_down, ln3_g, ln3_b):
    wi = w_in[l]
    d = wi.shape[0]
    o_kr = Q_RANK + KV_RANK
    o_fq = o_kr + ROPE
    half = ROPE // 2
    kr = wi[:, o_kr:o_fq]
    kr_sw = jnp.concatenate([kr[:, half:], kr[:, :half]], axis=1)
    n_rep = LANES // ROPE
    w_aug = jnp.concatenate([
        wi[:, :o_kr], jnp.tile(kr, (1, n_rep)), jnp.tile(kr_sw, (1, n_rep)), wi[:, o_fq:o_fq + 3 * FOX_W],
        jnp.pad(wi[:, o_fq + 3 * FOX_W:], ((0, 0), (0, LANES - H_FOX)))], axis=1).astype(BF16)
    uq = w_uq[l]
    uq_rope = uq[:, :, NOPE:]
    uq_sw = jnp.concatenate([uq_rope[:, :, half:], uq_rope[:, :, :half]], axis=2)
    w_uq_aug = jnp.concatenate([uq[:, :, :NOPE].reshape(Q_RANK, -1), uq_rope.reshape(Q_RANK, -1),
                                uq_sw.reshape(Q_RANK, -1)], axis=1).astype(BF16)
    uk_t = jnp.transpose(w_uk[l], (1, 2, 0))
    parity = jax.nn.one_hot(jnp.arange(H_MLA) % 2, 2, dtype=F32)
    w_uk_pad = (parity[:, :, None, None] * uk_t[:, None]).reshape(H_MLA, 2 * NOPE, KV_RANK).astype(BF16)
    w_uv_bd = jnp.einsum("rhv,hg->hrgv", w_uv[l], jnp.eye(H_MLA, dtype=F32)).reshape(
        H_MLA * KV_RANK, H_MLA * V_DIM).astype(BF16)
    lane = jnp.arange(LANES)
    inv = ROPE_BASE ** (-jnp.arange(half, dtype=F32) / half)
    row = lambda v: v.reshape(1, -1).astype(F32)
    return dict(
        w_in=w_aug, b_forget=row(jnp.pad(b_forget[l], (0, LANES - H_FOX))), g_q=row(g_q_lat[l]),
        g_kv=row(g_kv_lat[l]), w_uq=w_uq_aug, w_uk=w_uk_pad, inv_freq=row(inv[lane % half]),
        rope_sign=row(jnp.where(lane % ROPE < half, -1.0, 1.0)),
        w_uv=w_uv_bd, g_mla_out=row(g_mla_out[l]), g_fox_out=row(g_fox_out[l]), w_o=w_o[l].astype(BF16),
        ln1_g=row(ln1_g[l]), ln1_b=row(ln1_b[l]), w_mq=w_mq[l].reshape(d, d).astype(BF16),
        w_mk=w_mk[l].reshape(d, d).astype(BF16), w_mv=w_mv[l].reshape(d, d).astype(BF16),
        w_mo=w_mo[l].reshape(d, d).astype(BF16), ln2_g=row(ln2_g[l]), ln2_b=row(ln2_b[l]),
        w_up=w_up[l].astype(BF16), conv_w=conv_w[l].astype(F32), conv_b=row(conv_b[l]),
        w_down=w_down[l].astype(BF16), ln3_g=row(ln3_g[l]), ln3_b=row(ln3_b[l]))


def _constants(tm):
    tri = (np.arange(tm)[:, None] >= np.arange(tm)[None, :]).astype(np.float32)
    place_q = np.zeros((3, H_FOX, LANES), np.float32)
    place_k = np.zeros((3, H_FOX, LANES), np.float32)
    ones_q = np.zeros((1, LANES), np.float32)
    ones_k = np.zeros((1, LANES), np.float32)
    for hd in range(H_FOX):
        for j in range(3):
            place_q[j, hd, hd * 8 + j] = 1.0
            place_k[j, hd, hd * 8 + 3 + j] = -1.0
            ones_q[0, hd * 8 + 3 + j] = 1.0
            ones_k[0, hd * 8 + j] = 1.0
    head_mask = (np.arange(FOX_W)[None, :] // D_FOX == np.arange(H_FOX)[:, None]).astype(np.float32)
    return dict(tri=jnp.asarray(tri, BF16), place_q=jnp.asarray(place_q, BF16),
                place_k=jnp.asarray(place_k, BF16), ones_q=jnp.asarray(ones_q), ones_k=jnp.asarray(ones_k),
                triu_page=jnp.asarray(tri[:PAGE, :PAGE].T, BF16), head_mask=jnp.asarray(head_mask, BF16))


TOKEN_TILE = 512
MLA_Q_TILE = 512
MLA_Q_TILE_FUSED = 256
FFN_DEC_TILE = 256
DECODE_PAGES = 16


def _tile(n, pref):
    return pref if n % pref == 0 else n


def _layer(l, xp, xs, pos_p, pos_s, mem, caches, page_table, lw, consts):
    b, t, d = xp.shape
    tm = _tile(t, TOKEN_TILE)
    qcat, kcat, vcat, ckv, kr, fq, fk, fv, lf = _inproj(xp, pos_p, lw, tm)
    fqc, fkc, fvb = _foxprep(fq, fk, fv, lf, consts, tm)
    sample = _sample_front(xs, pos_s, lw, consts)
    pools = (caches[2], caches[3], caches[4], caches[0], caches[1])
    pps = _tile(page_table.shape[1], DECODE_PAGES)
    n_dec = page_table.shape[0] * (page_table.shape[1] // pps)
    tq = _tile(t, MLA_Q_TILE_FUSED)
    if b * len(_tile_pairs(t, tq, tm)[0]) >= n_dec:
        o_lat, o_fox_t, o_lat_s = _mla_flash_decode(l, qcat, kcat, vcat, tq, tm, page_table, pools,
                                                    sample["fox_new"], sample["mla_new"], consts, pps)
    else:
        o_lat = _mla_flash(qcat, kcat, vcat, _tile(t, MLA_Q_TILE), tm)
        o_fox_t, o_lat_s = _decode(l, page_table, pools, sample["fox_new"], sample["mla_new"], consts, pps)
    o_fox = _fox_flash(fqc, fkc, fvb, tm)
    y_p, st_p = _prompt_back(xp, o_lat, o_fox, ckv, kr, fk, fv, lf, mem, lw, tm)
    y_s, st_s = _sample_back(l, xs, o_fox_t, o_lat_s, sample, caches, lw)
    return y_p, st_p, y_s, st_s


def _prompt_back(x, o_lat, o_fox, ckv, kr, fk, fv, lf, mem, lw, tm):
    b, t, d = x.shape
    x1, mq = _outproj(x, o_lat, o_fox, lw, tm)
    mk, mv = _memkv(mem.reshape(b * MEM_TOK, d), lw["w_mk"], lw["w_mv"])
    mk = mk.reshape(b, MEM_TOK, d)
    mv = mv.reshape(b, MEM_TOK, d)
    mo = _cross(mq, mk, mv, tm)
    y, u_tail = _ffn(x1, mo, lw, tm)
    state = (ckv, kr, fk.reshape(b, t, H_FOX, D_FOX), fv.reshape(b, t, H_FOX, D_FOX), lf,
             mk.reshape(b, MEM_TOK, H_MEM, D_MEM), mv.reshape(b, MEM_TOK, H_MEM, D_MEM),
             u_tail[:, -1, SUBLANES - 2:, :])
    return y, state


def _sample_front(x, pos, lw, consts):
    bs, ts, d = x.shape
    n = bs * ts
    tm = _tile(n, TOKEN_TILE)
    xf = x.reshape(1, n, d)
    qcat, _, _, ckv, kr, fq, fk, fv, lf = _inproj(xf, pos, lw, tm)
    q4 = qcat[0].reshape(H_MLA, bs, ts, 256)
    qa = jnp.transpose(q4[..., :KV_RANK], (1, 0, 2, 3)).reshape(bs, H_MLA * ts, KV_RANK)
    qr = jnp.stack([q4[hd, :, :, KV_RANK + (hd % 4) * ROPE:KV_RANK + (hd % 4 + 1) * ROPE]
                    for hd in range(H_MLA)], axis=1).reshape(bs, H_MLA * ts, ROPE)
    keys_last = lambda a: jnp.pad(jnp.swapaxes(a.reshape(bs, ts, -1), 1, 2), ((0, 0), (0, 0), (0, PAGE - ts)))
    ckn = jnp.pad(ckv.reshape(bs, ts, KV_RANK), ((0, 0), (0, PAGE - ts), (0, 0)))
    qbd = (fq.reshape(bs, ts, 1, FOX_W) * consts["head_mask"][None, None]).reshape(bs, ts * H_FOX, FOX_W)
    return dict(fox_new=(qbd, keys_last(fk), keys_last(fv), keys_last(lf)), mla_new=(qa, qr, ckn, keys_last(kr)),
                ckv=ckv, kr=kr, fk=fk, fv=fv, lf=lf)


def _sample_back(l, x, o_fox_t, o_lat, sample, caches, lw):
    bs, ts, d = x.shape
    n = bs * ts
    tm = _tile(n, TOKEN_TILE)
    xf = x.reshape(1, n, d)
    c_mk, c_mv, c_conv = caches[5:]
    ckv, kr, fk, fv, lf = (sample[k] for k in ("ckv", "kr", "fk", "fv", "lf"))
    o_lat = jnp.transpose(o_lat.reshape(bs, H_MLA, ts, KV_RANK), (0, 2, 1, 3)).reshape(1, n, H_MLA * KV_RANK)
    o_fox = jnp.transpose(o_fox_t.reshape(bs, D_FOX, ts, H_FOX), (0, 2, 3, 1)).reshape(1, n, FOX_W)
    fk4 = fk.reshape(bs, ts, H_FOX, D_FOX)
    fv4 = fv.reshape(bs, ts, H_FOX, D_FOX)
    lf3 = lf.reshape(bs, ts, H_FOX)
    x1, mq = _outproj(xf, o_lat, o_fox, lw, tm)
    mo = _cross_seq(mq.reshape(bs, ts, d), c_mk, c_mv, l).reshape(1, n, d)
    pre = jnp.pad(c_conv[l], ((0, 0), (0, ts - c_conv.shape[2]), (0, 0))).reshape(1, n, -1)
    y, u = _ffn(x1, mo, lw, _tile(n, FFN_DEC_TILE), pre=pre)
    state = (ckv.reshape(bs, ts, KV_RANK), kr.reshape(bs, ts, ROPE), fk4, fv4, lf3,
             u.reshape(bs, ts, -1)[:, ts - 2:, :])
    return y.reshape(bs, ts, d), state


def kernel(x_prompt, x_sample, mem_prompt, cache_mla_ckv, cache_mla_krope, cache_fox_k, cache_fox_v, cache_fox_logf, cache_mem_k, cache_mem_v, state_conv, page_table, w_in, b_forget, g_q_lat, g_kv_lat, w_uq, w_uk, w_uv, g_mla_out, g_fox_out, w_o, ln1_g, ln1_b, w_mq, w_mk, w_mv, w_mo, ln2_g, ln2_b, w_up, conv_w, conv_b, w_down, ln3_g, ln3_b):
    bp, tp, _ = x_prompt.shape
    bs, ts, _ = x_sample.shape
    n_past = page_table.shape[1] * PAGE
    pos_p = jnp.broadcast_to(jnp.arange(tp, dtype=F32)[None, :, None], (bp, tp, 1))
    pos_s = (n_past + jnp.arange(bs * ts) % ts).astype(F32).reshape(1, bs * ts, 1)
    consts = _constants(_tile(tp, TOKEN_TILE))
    n_pool = cache_fox_k.shape[1]
    kv_view = lambda a: jnp.transpose(a, (0, 1, 3, 4, 2)).reshape(DEPTH, n_pool, FOX_W, PAGE)
    caches = (cache_mla_ckv, jnp.swapaxes(cache_mla_krope, 2, 3), kv_view(cache_fox_k), kv_view(cache_fox_v),
              jnp.swapaxes(cache_fox_logf, 2, 3), cache_mem_k, cache_mem_v, state_conv)
    weights = (w_in, b_forget, g_q_lat, g_kv_lat, w_uq, w_uk, w_uv, g_mla_out, g_fox_out, w_o, ln1_g, ln1_b,
               w_mq, w_mk, w_mv, w_mo, ln2_g, ln2_b, w_up, conv_w, conv_b, w_down, ln3_g, ln3_b)
    xp, xs = x_prompt, x_sample
    p_states, s_states = [], []
    for l in range(DEPTH):
        lw = _layer_weights(l, *weights)
        xp, st_p, xs, st_s = _layer(l, xp, xs, pos_p, pos_s, mem_prompt, caches, page_table, lw, consts)
        p_states.append(st_p)
        s_states.append(st_s)
    stack = lambda states, i: jnp.stack([st[i] for st in states])
    return ((xp, xs) + tuple(stack(p_states, i) for i in range(8)) + tuple(stack(s_states, i) for i in range(6)))
```

```python
import functools

import numpy as np
import jax
import jax.numpy as jnp
from jax import lax
from jax.experimental import pallas as pl
from jax.experimental.pallas import tpu as pltpu

F32 = jnp.float32
BF16 = jnp.bfloat16

DEPTH = 2
PAGE = 128
H_MLA = 8
NOPE = 64
ROPE = 32
V_DIM = 64
Q_RANK = 256
KV_RANK = 128
MLA_SCALE = (NOPE + ROPE) ** -0.5
ROPE_BASE = 10000.0
H_FOX = 8
D_FOX = 64
FOX_W = H_FOX * D_FOX
FOX_SCALE = D_FOX ** -0.5
MEM_TOK = 256
H_MEM = 4
D_MEM = 256
MEM_SCALE = D_MEM ** -0.5
LN_EPS = 1e-5
RMS_EPS = 1e-6
ALPHA = (2 * DEPTH) ** 0.25

LANES = 128
SUBLANES = 8
VMEM_LIMIT = 56 * 1024 * 1024

C_Q, C_KV, C_KR, C_KS, C_FQ, C_FK, C_FV, C_FL, C_END = 0, 256, 384, 512, 640, 1152, 1664, 2176, 2304

NEG_INF = float("-inf")


def _mm(a, b):
    return jnp.dot(a, b, preferred_element_type=F32)


def _mm_nt(a, b):
    return lax.dot_general(a, b, (((1,), (1,)), ((), ())), preferred_element_type=F32)


def _mm_tn(a, b):
    return lax.dot_general(a, b, (((0,), (0,)), ((), ())), preferred_element_type=F32)


def _rms(x, g):
    return x * lax.rsqrt(jnp.mean(x * x, axis=-1, keepdims=True) + RMS_EPS) * g


def _ln(x, g, b):
    mu = jnp.mean(x, axis=-1, keepdims=True)
    xc = x - mu
    var = jnp.mean(xc * xc, axis=-1, keepdims=True)
    return xc * lax.rsqrt(var + LN_EPS) * g + b


def _split3(x):
    hi = x.astype(BF16)
    r = x - hi.astype(F32)
    mid = r.astype(BF16)
    lo = (r - mid.astype(F32)).astype(BF16)
    return hi, mid, lo


def _const_spec(shape):
    nd = len(shape)
    return pl.BlockSpec(shape, lambda *_: (0,) * nd)


def _params(sem):
    return pltpu.CompilerParams(dimension_semantics=sem, vmem_limit_bytes=VMEM_LIMIT)


def _inproj_kernel(x_ref, pos_ref, w_ref, bf_ref, gq_ref, gkv_ref, wuq_ref, wuk_ref, inv_ref, sgn_ref,
                   qcat_ref, kcat_ref, vcat_ref, ckv_ref, kr_ref, fq_ref, fk_ref, fv_ref, lf_ref):
    xb = x_ref[0].astype(BF16)
    h = _mm(xb, w_ref[...])
    ang = pos_ref[0] * inv_ref[...]
    cos = jnp.cos(ang)
    sin = jnp.sin(ang) * sgn_ref[...]
    qn = _rms(h[:, C_Q:C_KV], gq_ref[...]).astype(BF16)
    qu = _mm(qn, wuq_ref[...])
    ckv = _rms(h[:, C_KV:C_KR], gkv_ref[...])
    kr = h[:, C_KR:C_KS] * cos + h[:, C_KS:C_FQ] * sin
    ckv_ref[0] = ckv
    kr_ref[0] = kr[:, :ROPE]
    kcat_ref[0] = jnp.concatenate([ckv.astype(BF16), kr.astype(BF16)], axis=1)
    vcat_ref[0] = jnp.concatenate([ckv.astype(BF16), jnp.ones(ckv.shape, BF16)], axis=1)
    lane =lax.broadcasted_iota(jnp.int32, (1, LANES), 1)
    slabs = []
    for g in range(2):
        lo = 512 + g * LANES
        slabs.append(qu[:, lo:lo + LANES] * cos + qu[:, lo + 256:lo + 256 + LANES] * sin)
    for hd in range(H_MLA):
        g2 = hd // 2
        qabs = _mm(qu[:, g2 * LANES:(g2 + 1) * LANES].astype(BF16), wuk_ref[hd])
        slab = jnp.where(lane // ROPE == hd % 4, slabs[hd // 4], 0.0)
        qcat_ref[0, hd] = jnp.concatenate(
            [(qabs * MLA_SCALE).astype(BF16), (slab * MLA_SCALE).astype(BF16)], axis=1)
    fq_ref[0] = (h[:, C_FQ:C_FK] * FOX_SCALE).astype(BF16)
    fk_ref[0] = h[:, C_FK:C_FV]
    fv_ref[0] = h[:, C_FV:C_FL]
    z = h[:, C_FL:C_END] + bf_ref[...]
    logf = jnp.minimum(z, 0.0) - jnp.log1p(jnp.exp(-jnp.abs(z)))
    lf_ref[0] = logf[:, :H_FOX]


def _inproj(x, pos, lw, tm):
    b, t, d = x.shape
    grid = (b, t // tm)
    tok = lambda w: pl.BlockSpec((1, tm, w), lambda i, j: (i, j, 0))
    out_shape = (
        jax.ShapeDtypeStruct((b, H_MLA, t, 256), BF16),
        jax.ShapeDtypeStruct((b, t, 256), BF16),
        jax.ShapeDtypeStruct((b, t, 256), BF16),
        jax.ShapeDtypeStruct((b, t, KV_RANK), F32),
        jax.ShapeDtypeStruct((b, t, ROPE), F32),
        jax.ShapeDtypeStruct((b, t, FOX_W), BF16),
        jax.ShapeDtypeStruct((b, t, FOX_W), F32),
        jax.ShapeDtypeStruct((b, t, FOX_W), F32),
        jax.ShapeDtypeStruct((b, t, H_FOX), F32),
    )
    out_specs = (
        pl.BlockSpec((1, H_MLA, tm, 256), lambda i, j: (i, 0, j, 0)),
        tok(256), tok(256), tok(KV_RANK), tok(ROPE), tok(FOX_W), tok(FOX_W), tok(FOX_W), tok(H_FOX),
    )
    in_specs = [
        tok(d), tok(1),
        _const_spec(lw["w_in"].shape), _const_spec((1, LANES)), _const_spec((1, Q_RANK)),
        _const_spec((1, KV_RANK)), _const_spec(lw["w_uq"].shape), _const_spec(lw["w_uk"].shape),
        _const_spec((1, LANES)), _const_spec((1, LANES)),
    ]
    return pl.pallas_call(
        _inproj_kernel, grid=grid, in_specs=in_specs, out_specs=out_specs, out_shape=out_shape,
        compiler_params=_params(("parallel", "parallel")), name="inproj",
    )(x, pos, lw["w_in"], lw["b_forget"], lw["g_q"], lw["g_kv"], lw["w_uq"], lw["w_uk"],
      lw["inv_freq"], lw["rope_sign"])


def _foxprep_kernel(fq_ref, fk_ref, fv_ref, lf_ref, tri_ref, pq_ref, pk_ref, oq_ref, ok_ref,
                    qcat_ref, kcat_ref, vb_ref, carry_ref):
    tm = lf_ref.shape[1]

    @pl.when(pl.program_id(1) == 0)
    def _():
        carry_ref[...] = jnp.zeros_like(carry_ref)

    tri = tri_ref[...]
    c = carry_ref[...]
    for part in _split3(lf_ref[0]):
        c = c + _mm(tri, part)
    carry_ref[...] = c[tm - 1:tm, :]
    parts = _split3(c)
    augq = oq_ref[...]
    augk = ok_ref[...]
    for j in range(3):
        augq = augq + _mm(parts[j], pq_ref[j])
        augk = augk + _mm(parts[j], pk_ref[j])
    augk = augk.astype(BF16)
    lane = lax.broadcasted_iota(jnp.int32, (1, LANES), 1)
    fq = fq_ref[0]
    fkb = fk_ref[0].astype(BF16)
    fvb = fv_ref[0].astype(BF16)
    for g in range(H_FOX // 2):
        kcat_ref[0, g] = jnp.concatenate([fkb[:, g * LANES:(g + 1) * LANES], augk], axis=1)
        vb_ref[0, g] = jnp.concatenate([fvb[:, g * LANES:(g + 1) * LANES], jnp.ones((tm, LANES), BF16)], axis=1)
    for hd in range(H_FOX):
        g = hd // 2
        qs = jnp.where(lane // D_FOX == hd % 2, fq[:, g * LANES:(g + 1) * LANES].astype(F32), 0.0)
        aq = jnp.where(lane // 8 == hd, augq, 0.0)
        qcat_ref[0, hd] = jnp.concatenate([qs.astype(BF16), aq.astype(BF16)], axis=1)


def _foxprep(fq, fk, fv, lf, consts, tm):
    b, t, _ = fq.shape
    tok = lambda w: pl.BlockSpec((1, tm, w), lambda i, j: (i, j, 0))
    out_shape = (
        jax.ShapeDtypeStruct((b, H_FOX, t, 256), BF16),
        jax.ShapeDtypeStruct((b, H_FOX // 2, t, 256), BF16),
        jax.ShapeDtypeStruct((b, H_FOX // 2, t, 256), BF16),
    )
    out_specs = (
        pl.BlockSpec((1, H_FOX, tm, 256), lambda i, j: (i, 0, j, 0)),
        pl.BlockSpec((1, H_FOX // 2, tm, 256), lambda i, j: (i, 0, j, 0)),
        pl.BlockSpec((1, H_FOX // 2, tm, 256), lambda i, j: (i, 0, j, 0)),
    )
    in_specs = [tok(FOX_W), tok(FOX_W), tok(FOX_W), tok(H_FOX),
                _const_spec((tm, tm)), _const_spec((3, H_FOX, LANES)), _const_spec((3, H_FOX, LANES)),
                _const_spec((1, LANES)), _const_spec((1, LANES))]
    return pl.pallas_call(
        _foxprep_kernel, grid=(b, t // tm), in_specs=in_specs, out_specs=out_specs, out_shape=out_shape,
        scratch_shapes=[pltpu.VMEM((1, H_FOX), F32)],
        compiler_params=_params(("parallel", "arbitrary")), name="foxprep",
    )(fq, fk, fv, lf, consts["tri"], consts["place_q"], consts["place_k"], consts["ones_q"], consts["ones_k"])


def _tile_pairs(t, tq, tk):
    qi, kj = [], []
    for i in range(t // tq):
        for j in range(((i + 1) * tq - 1) // tk + 1):
            qi.append(i)
            kj.append(j)
    return jnp.asarray(np.array(qi, np.int32)), jnp.asarray(np.array(kj, np.int32))


def _flash_step(s, v_ones, m_ref, acc_ref, idx=Ellipsis):
    m_old = m_ref[idx]
    m_new = jnp.maximum(m_old, jnp.max(s, axis=1, keepdims=True))
    alpha = jnp.exp(m_old - m_new)
    pr = jnp.exp(s - jnp.concatenate([m_new] * (s.shape[1] // LANES), axis=1))
    acc_ref[idx] = jnp.concatenate([alpha, alpha], axis=1) * acc_ref[idx] + _mm(pr.astype(BF16), v_ones)
    m_ref[idx] = m_new


def _mla_flash_kernel(qi_ref, kj_ref, q_ref, k_ref, v_ref, o_ref, m_ref, acc_ref, *, tq, tk):
    p = pl.program_id(1)
    qi = qi_ref[p]
    kj = kj_ref[p]
    rows = H_MLA * tq

    @pl.when(kj == 0)
    def _():
        m_ref[...] = jnp.full_like(m_ref, NEG_INF)
        acc_ref[...] = jnp.zeros_like(acc_ref)

    def update(masked):
        s = _mm_nt(q_ref[0].reshape(rows, 256), k_ref[0])
        if masked:
            row = qi * tq + lax.broadcasted_iota(jnp.int32, (rows, tk), 0) % tq
            col = kj * tk + lax.broadcasted_iota(jnp.int32, (rows, tk), 1)
            s = jnp.where(col <= row, s, NEG_INF)
        _flash_step(s, v_ref[0], m_ref, acc_ref)

    straddle = kj * tk + tk - 1 > qi * tq
    pl.when(straddle)(functools.partial(update, True))
    pl.when(jnp.logical_not(straddle))(functools.partial(update, False))

    @pl.when((kj + 1) * tk >= (qi + 1) * tq)
    def _():
        acc = acc_ref[...]
        o = acc[:, :KV_RANK] / acc[:, KV_RANK:]
        for hd in range(H_MLA):
            o_ref[0, :, hd * KV_RANK:(hd + 1) * KV_RANK] = o[hd * tq:(hd + 1) * tq]


def _mla_flash(qcat, kcat, vcat, tq, tk):
    b, _, t, _ = qcat.shape
    qi, kj = _tile_pairs(t, tq, tk)
    kv_spec = pl.BlockSpec((1, tk, 256), lambda i, p, qi, kj: (i, kj[p], 0))
    grid_spec = pltpu.PrefetchScalarGridSpec(
        num_scalar_prefetch=2, grid=(b, int(qi.shape[0])),
        in_specs=[pl.BlockSpec((1, H_MLA, tq, 256), lambda i, p, qi, kj: (i, 0, qi[p], 0)), kv_spec, kv_spec],
        out_specs=pl.BlockSpec((1, tq, H_MLA * KV_RANK), lambda i, p, qi, kj: (i, qi[p], 0)),
        scratch_shapes=[pltpu.VMEM((H_MLA * tq, LANES), F32), pltpu.VMEM((H_MLA * tq, 2 * KV_RANK), F32)])
    return pl.pallas_call(
        functools.partial(_mla_flash_kernel, tq=tq, tk=tk), grid_spec=grid_spec,
        out_shape=jax.ShapeDtypeStruct((b, t, H_MLA * KV_RANK), F32),
        compiler_params=_params(("parallel", "arbitrary")), name="mla_flash",
    )(qi, kj, qcat, kcat, vcat)


def _fox_flash_kernel(qi_ref, kj_ref, q_ref, k_ref, v_ref, o_ref, m_ref, acc_ref, *, t):
    p = pl.program_id(1)
    qi = qi_ref[p]
    kj = kj_ref[p]

    @pl.when(kj == 0)
    def _():
        m_ref[...] = jnp.full_like(m_ref, NEG_INF)
        acc_ref[...] = jnp.zeros_like(acc_ref)

    def heads(masked):
        def body(g, carry):
            q2 = q_ref[0, pl.ds(2 * g, 2)].reshape(2 * t, 256)
            s = _mm_nt(q2, k_ref[0, g])
            if masked:
                row = lax.broadcasted_iota(jnp.int32, (2 * t, t), 0) % t
                col = lax.broadcasted_iota(jnp.int32, (2 * t, t), 1)
                s = jnp.where(col <= row, s, NEG_INF)
            _flash_step(s, v_ref[0, g], m_ref, acc_ref, g)
            return carry
        lax.fori_loop(0, H_FOX // 2, body, 0, unroll=2)

    pl.when(kj == qi)(functools.partial(heads, True))
    pl.when(kj != qi)(functools.partial(heads, False))

    @pl.when(kj == qi)
    def _():
        lane = lax.broadcasted_iota(jnp.int32, (1, LANES), 1)
        for g in range(H_FOX // 2):
            oa = acc_ref[g, :t, :LANES] / acc_ref[g, :t, LANES:]
            ob = acc_ref[g, t:, :LANES] / acc_ref[g, t:, LANES:]
            o_ref[0, :, g * LANES:(g + 1) * LANES] = jnp.where(lane < D_FOX, oa, ob)


def _fox_flash(qcat, kcat, vb, t_blk):
    b, _, t, _ = qcat.shape
    qi, kj = _tile_pairs(t, t_blk, t_blk)
    grid_spec = pltpu.PrefetchScalarGridSpec(
        num_scalar_prefetch=2, grid=(b, int(qi.shape[0])),
        in_specs=[pl.BlockSpec((1, H_FOX, t_blk, 256), lambda i, p, qi, kj: (i, 0, qi[p], 0)),
                  pl.BlockSpec((1, H_FOX // 2, t_blk, 256), lambda i, p, qi, kj: (i, 0, kj[p], 0)),
                  pl.BlockSpec((1, H_FOX // 2, t_blk, 256), lambda i, p, qi, kj: (i, 0, kj[p], 0))],
        out_specs=pl.BlockSpec((1, t_blk, FOX_W), lambda i, p, qi, kj: (i, qi[p], 0)),
        scratch_shapes=[pltpu.VMEM((H_FOX // 2, 2 * t_blk, LANES), F32),
                        pltpu.VMEM((H_FOX // 2, 2 * t_blk, 2 * LANES), F32)])
    return pl.pallas_call(
        functools.partial(_fox_flash_kernel, t=t_blk), grid_spec=grid_spec,
        out_shape=jax.ShapeDtypeStruct((b, t, FOX_W), F32),
        compiler_params=_params(("parallel", "arbitrary")), name="fox_flash",
    )(qi, kj, qcat, kcat, vb)


def _softmax_step(s, m_ref, l_ref):
    m_old = m_ref[...]
    m_new = jnp.maximum(m_old, jnp.max(s, axis=1, keepdims=True))
    alpha = jnp.exp(m_old - m_new)
    pr = jnp.exp(s - m_new)
    l_ref[...] = alpha * l_ref[...] + jnp.sum(pr, axis=1, keepdims=True)
    m_ref[...] = m_new
    return pr, alpha


def _mla_decode_phases(ckv_pages, krt_pages, qa_ref, qr_ref, ckn_ref, krn_ref, o_ref, m_ref, l_ref, acc_ref, *,
                       n_new):
    def init():
        m_ref[...] = jnp.full_like(m_ref, NEG_INF)
        l_ref[...] = jnp.zeros_like(l_ref)
        acc_ref[...] = jnp.zeros_like(acc_ref)

    def update(ckv, krt, mask):
        qa = qa_ref[0]
        s = _mm_nt(qa, ckv) + _mm(qr_ref[0], krt)
        if mask is not None:
            s = jnp.where(mask, s, NEG_INF)
        pr, alpha = _softmax_step(s, m_ref, l_ref)
        acc_ref[...] = alpha * acc_ref[...] + _mm(pr.astype(BF16), ckv)

    def pages():
        ckv = jnp.concatenate([r[...].astype(BF16) for r in ckv_pages], axis=0)
        krt = jnp.concatenate([r[...].astype(BF16) for r in krt_pages], axis=1)
        update(ckv, krt, None)

    def finish():
        rows = qa_ref.shape[1]
        key = lax.broadcasted_iota(jnp.int32, (rows, PAGE), 1)
        qry = lax.broadcasted_iota(jnp.int32, (rows, PAGE), 0) % n_new
        update(ckn_ref[0].astype(BF16), krn_ref[0].astype(BF16), key <= qry)
        o_ref[0] = acc_ref[...] / l_ref[...]

    return init, pages, finish


def _fox_decode_phases(kt_pages, vt_pages, lf_pages, qbd_ref, triu_ref, knt_ref, vnt_ref, lfn_ref, o_ref,
                       m_ref, l_ref, acc_ref, carry_ref, *, n_new):
    rows = n_new * H_FOX

    def init():
        m_ref[...] = jnp.full_like(m_ref, NEG_INF)
        l_ref[...] = jnp.zeros_like(l_ref)
        acc_ref[...] = jnp.zeros_like(acc_ref)
        carry_ref[...] = jnp.zeros_like(carry_ref)

    def col_to_row(x):
        return jnp.transpose(jnp.broadcast_to(x, (rows, LANES)))[0:1, :]

    def update(kt, vt, lf_rows, mask):
        qbd = qbd_ref[0]
        triu = triu_ref[...]
        n = lf_rows.shape[0] // H_FOX
        cw = jnp.zeros(lf_rows.shape, F32)
        for part in _split3(lf_rows):
            cw = cw + _mm(part, triu)
        tot = jnp.sum(lf_rows, axis=1, keepdims=True)
        carry = carry_ref[...]
        bias = []
        for j in range(n):
            bias.append(cw[j * H_FOX:(j + 1) * H_FOX] + carry)
            carry = carry + tot[j * H_FOX:(j + 1) * H_FOX]
        carry_ref[...] = carry
        b = jnp.concatenate(bias, axis=1) if n > 1 else bias[0]
        s = _mm(qbd, kt)
        s = (s.reshape(n_new, H_FOX, s.shape[1]) - b[None]).reshape(rows, s.shape[1])
        if mask is not None:
            s = jnp.where(mask, s, NEG_INF)
        pr, alpha = _softmax_step(s, m_ref, l_ref)
        acc_ref[...] = col_to_row(alpha) * acc_ref[...] + _mm_nt(vt, pr.astype(BF16))

    def pages():
        update(jnp.concatenate([r[...].astype(BF16) for r in kt_pages], axis=1),
               jnp.concatenate([r[...].astype(BF16) for r in vt_pages], axis=1),
               jnp.concatenate([r[...] for r in lf_pages], axis=0), None)

    def finish():
        key = lax.broadcasted_iota(jnp.int32, (rows, PAGE), 1)
        qry = lax.broadcasted_iota(jnp.int32, (rows, PAGE), 0) // H_FOX
        update(knt_ref[0].astype(BF16), vnt_ref[0].astype(BF16), lfn_ref[0], key <= qry)
        lane = lax.broadcasted_iota(jnp.int32, (1, rows), 1) % H_FOX
        o = jnp.zeros((D_FOX, rows), F32)
        for hd in range(H_FOX):
            o = o + jnp.where(lane == hd, acc_ref[hd * D_FOX:(hd + 1) * D_FOX, :], 0.0)
        o_ref[0] = o / col_to_row(l_ref[...])

    return init, pages, finish


N_DECODE_STREAMS = 5
N_DECODE_SEQ_INPUTS = 9
N_DECODE_STATE = 7


def _decode_phases(step, last, pt_ref, hbms, seq_refs, out_refs, bufs, sem, state, *, layer, pps, nchunk, n_new):
    (qbd_ref, triu_ref, knt_ref, vnt_ref, lfn_ref, qa_ref, qr_ref, ckn_ref, krn_ref) = seq_refs
    of_ref, ol_ref = out_refs
    fm_ref, fl_ref, facc_ref, carry_ref, mm_ref, ml_ref, macc_ref = state
    n_slots = bufs[0].shape[0]
    ahead = n_slots - 1
    c = step % nchunk
    slot = step % n_slots
    streams = tuple(zip(hbms, bufs))

    def copies(v):
        st = jnp.minimum(v, last)
        sl = v % n_slots
        out = []
        for j in range(pps):
            page = pt_ref[st * pps + j]
            for k, (hbm, buf) in enumerate(streams):
                out.append(pltpu.make_async_copy(hbm.at[layer, page], buf.at[sl, j], sem.at[sl, k]))
        return out

    views = [[buf.at[slot, j] for j in range(pps)] for _, buf in streams]
    mla = _mla_decode_phases(views[3], views[4], qa_ref, qr_ref, ckn_ref, krn_ref, ol_ref, mm_ref, ml_ref,
                             macc_ref, n_new=n_new)
    fox = _fox_decode_phases(views[0], views[1], views[2], qbd_ref, triu_ref, knt_ref, vnt_ref, lfn_ref, of_ref,
                             fm_ref, fl_ref, facc_ref, carry_ref, n_new=n_new)

    def prologue():
        for v in range(ahead):
            for cp in copies(v):
                cp.start()

    def fetch():
        for cp in copies(step):
            cp.wait()
        for cp in copies(step + ahead):
            cp.start()

    def epilogue():
        for v in range(1, ahead + 1):
            for cp in copies(step + v):
                cp.wait()

    def both(i):
        def run():
            mla[i]()
            fox[i]()
        return run

    return c, prologue, fetch, both(0), both(1), both(2), epilogue


def _decode_kernel(pt_ref, *refs, layer, pps, nchunk, n_new):
    ns, nq = N_DECODE_STREAMS, N_DECODE_SEQ_INPUTS
    step = pl.program_id(0) * nchunk + pl.program_id(1)
    last = pl.num_programs(0) * nchunk - 1
    c, prologue, fetch, init, pages, finish, epilogue = _decode_phases(
        step, last, pt_ref, refs[:ns], refs[ns:ns + nq], refs[ns + nq:ns + nq + 2],
        refs[ns + nq + 2:2 * ns + nq + 2], refs[2 * ns + nq + 2], refs[2 * ns + nq + 3:],
        layer=layer, pps=pps, nchunk=nchunk, n_new=n_new)
    pl.when(step == 0)(prologue)
    pl.when(c == 0)(init)
    fetch()
    pages()
    pl.when(c == nchunk - 1)(finish)
    pl.when(step == last)(epilogue)


def _decode_operands(page_table, pps, seq_index, n_slots=3):
    n_new = 8
    rows = n_new * H_FOX
    page_shapes = ((FOX_W, PAGE), (FOX_W, PAGE), (H_FOX, PAGE), (PAGE, KV_RANK), (ROPE, PAGE))
    hbm = pl.BlockSpec(memory_space=pl.ANY)
    per_seq = lambda shape: pl.BlockSpec((1,) + shape, lambda *a: (seq_index(*a),) + (0,) * len(shape))
    const = lambda shape: pl.BlockSpec(shape, lambda *a: (0,) * len(shape))
    in_specs = ([hbm] * len(page_shapes)
                + [per_seq((rows, FOX_W)), const((PAGE, PAGE)),
                   per_seq((FOX_W, PAGE)), per_seq((FOX_W, PAGE)), per_seq((H_FOX, PAGE)),
                   per_seq((rows, KV_RANK)), per_seq((rows, ROPE)), per_seq((PAGE, KV_RANK)),
                   per_seq((ROPE, PAGE))])
    out_specs = [per_seq((D_FOX, rows)), per_seq((rows, KV_RANK))]
    bs = page_table.shape[0]
    out_shape = [jax.ShapeDtypeStruct((bs, D_FOX, rows), F32), jax.ShapeDtypeStruct((bs, rows, KV_RANK), F32)]
    scratch = ([pltpu.VMEM((n_slots, pps) + shape, F32) for shape in page_shapes]
               + [pltpu.SemaphoreType.DMA((n_slots, len(page_shapes))),
                  pltpu.VMEM((rows, 1), F32), pltpu.VMEM((rows, 1), F32),
                  pltpu.VMEM((FOX_W, rows), F32), pltpu.VMEM((H_FOX, 1), F32),
                  pltpu.VMEM((rows, 1), F32), pltpu.VMEM((rows, 1), F32), pltpu.VMEM((rows, KV_RANK), F32)])
    return in_specs, out_specs, out_shape, scratch, n_new


def _decode_args(pools, fox_new, mla_new, consts):
    qbd, knt, vnt, lfnt = fox_new
    qa, qr, ckn, krnt = mla_new
    return tuple(pools) + (qbd, consts["triu_page"], knt, vnt, lfnt, qa, qr, ckn, krnt)


def _decode(layer, page_table, pools, fox_new, mla_new, consts, pps):
    bs, npages = page_table.shape
    nchunk = npages // pps
    assert npages == nchunk * pps
    in_specs, out_specs, out_shape, scratch, n_new = _decode_operands(page_table, pps, lambda b, c, pt: b)
    grid_spec = pltpu.PrefetchScalarGridSpec(
        num_scalar_prefetch=1, grid=(bs, nchunk), in_specs=in_specs, out_specs=tuple(out_specs),
        scratch_shapes=scratch)
    return pl.pallas_call(
        functools.partial(_decode_kernel, layer=layer, pps=pps, nchunk=nchunk, n_new=n_new), grid_spec=grid_spec,
        out_shape=tuple(out_shape), compiler_params=_params(("arbitrary", "arbitrary")), name="decode",
    )(page_table.reshape(-1), *_decode_args(pools, fox_new, mla_new, consts))


def _flash_carrying_decode(f, flash, needs_mask, pt_ref, refs, n_flash_refs, *, layer, pps, nchunk, n_dec, n_new):
    ns, nq = N_DECODE_STREAMS, N_DECODE_SEQ_INPUTS
    n_fo, n_fs = n_flash_refs
    hbms, seq_refs = refs[:ns], refs[ns:ns + nq]
    dec_out = refs[ns + nq + n_fo:ns + nq + n_fo + 2]
    base = ns + nq + n_fo + 2 + n_fs
    bufs, sem, state = refs[base:base + ns], refs[base + ns], refs[base + ns + 1:]
    step = jnp.minimum(f, n_dec - 1)
    c, prologue, fetch, init, pages, finish, epilogue = _decode_phases(
        step, n_dec - 1, pt_ref, hbms, seq_refs, dec_out, bufs, sem, state,
        layer=layer, pps=pps, nchunk=nchunk, n_new=n_new)
    has_dec = f < n_dec
    pl.when(f == 0)(prologue)
    pl.when(jnp.logical_and(has_dec, c == 0))(init)
    for with_dec in (True, False):
        for masked in (True, False):
            def body(with_dec=with_dec, masked=masked):
                if with_dec:
                    fetch()
                    pages()
                flash(masked)
            pl.when(jnp.logical_and(has_dec == with_dec, needs_mask == masked))(body)
    pl.when(jnp.logical_and(has_dec, c == nchunk - 1))(finish)
    pl.when(f == n_dec - 1)(epilogue)


def _mla_flash_decode_kernel(qi_ref, kj_ref, pt_ref, q_ref, k_ref, v_ref, *refs, tq, tk, **dec):
    ns, nq = N_DECODE_STREAMS, N_DECODE_SEQ_INPUTS
    o_ref = refs[ns + nq]
    m_ref, acc_ref = refs[ns + nq + 3:ns + nq + 5]
    p = pl.program_id(1)
    f = pl.program_id(0) * pl.num_programs(1) + p
    qi = qi_ref[p]
    kj = kj_ref[p]
    rows = H_MLA * tq

    @pl.when(kj == 0)
    def _():
        m_ref[...] = jnp.full_like(m_ref, NEG_INF)
        acc_ref[...] = jnp.zeros_like(acc_ref)

    def flash(masked):
        s = _mm_nt(q_ref[0].reshape(rows, 256), k_ref[0])
        if masked:
            row = qi * tq + lax.broadcasted_iota(jnp.int32, (rows, tk), 0) % tq
            col = kj * tk + lax.broadcasted_iota(jnp.int32, (rows, tk), 1)
            s = jnp.where(col <= row, s, NEG_INF)
        _flash_step(s, v_ref[0], m_ref, acc_ref)

    _flash_carrying_decode(f, flash, kj * tk + tk - 1 > qi * tq, pt_ref, refs, (1, 2), **dec)

    @pl.when((kj + 1) * tk >= (qi + 1) * tq)
    def _():
        acc = acc_ref[...]
        o = acc[:, :KV_RANK] / acc[:, KV_RANK:]
        for hd in range(H_MLA):
            o_ref[0, :, hd * KV_RANK:(hd + 1) * KV_RANK] = o[hd * tq:(hd + 1) * tq]


def _fox_flash_decode_kernel(qi_ref, kj_ref, pt_ref, q_ref, k_ref, v_ref, *refs, t, **dec):
    ns, nq = N_DECODE_STREAMS, N_DECODE_SEQ_INPUTS
    o_ref = refs[ns + nq]
    m_ref, acc_ref = refs[ns + nq + 3:ns + nq + 5]
    p = pl.program_id(1)
    f = pl.program_id(0) * pl.num_programs(1) + p
    qi = qi_ref[p]
    kj = kj_ref[p]

    @pl.when(kj == 0)
    def _():
        m_ref[...] = jnp.full_like(m_ref, NEG_INF)
        acc_ref[...] = jnp.zeros_like(acc_ref)

    def flash(masked):
        for g in range(H_FOX // 2):
            s = _mm_nt(q_ref[0, 2 * g:2 * g + 2].reshape(2 * t, 256), k_ref[0, g])
            if masked:
                row = lax.broadcasted_iota(jnp.int32, (2 * t, t), 0) % t
                col = lax.broadcasted_iota(jnp.int32, (2 * t, t), 1)
                s = jnp.where(col <= row, s, NEG_INF)
            _flash_step(s, v_ref[0, g], m_ref, acc_ref, g)

    _flash_carrying_decode(f, flash, kj == qi, pt_ref, refs, (1, 2), **dec)

    @pl.when(kj == qi)
    def _():
        lane = lax.broadcasted_iota(jnp.int32, (1, LANES), 1)
        for g in range(H_FOX // 2):
            oa = acc_ref[g, :t, :LANES] / acc_ref[g, :t, LANES:]
            ob = acc_ref[g, t:, :LANES] / acc_ref[g, t:, LANES:]
            o_ref[0, :, g * LANES:(g + 1) * LANES] = jnp.where(lane < D_FOX, oa, ob)


def _fox_flash_decode(layer, qcat, kcat, vb, t_blk, page_table, pools, fox_new, mla_new, consts, pps):
    b, _, t, _ = qcat.shape
    qi, kj = _tile_pairs(t, t_blk, t_blk)
    npairs = int(qi.shape[0])
    bs, npages = page_table.shape
    nchunk = npages // pps
    n_dec = bs * nchunk
    assert npages == nchunk * pps and b * npairs >= n_dec
    seq = lambda i, p, qi, kj, pt: jnp.minimum(i * npairs + p, n_dec - 1) // nchunk
    d_in, d_out, d_shape, d_scratch, n_new = _decode_operands(page_table, pps, seq, n_slots=2)
    kv_spec = pl.BlockSpec((1, H_FOX // 2, t_blk, 256), lambda i, p, qi, kj, pt: (i, 0, kj[p], 0))
    grid_spec = pltpu.PrefetchScalarGridSpec(
        num_scalar_prefetch=3, grid=(b, npairs),
        in_specs=[pl.BlockSpec((1, H_FOX, t_blk, 256), lambda i, p, qi, kj, pt: (i, 0, qi[p], 0)), kv_spec,
                  kv_spec] + d_in,
        out_specs=tuple([pl.BlockSpec((1, t_blk, FOX_W), lambda i, p, qi, kj, pt: (i, qi[p], 0))] + d_out),
        scratch_shapes=[pltpu.VMEM((H_FOX // 2, 2 * t_blk, LANES), F32),
                        pltpu.VMEM((H_FOX // 2, 2 * t_blk, 2 * LANES), F32)] + d_scratch)
    return pl.pallas_call(
        functools.partial(_fox_flash_decode_kernel, t=t_blk, layer=layer, pps=pps, nchunk=nchunk, n_dec=n_dec,
                          n_new=n_new),
        grid_spec=grid_spec, out_shape=tuple([jax.ShapeDtypeStruct((b, t, FOX_W), F32)] + d_shape),
        compiler_params=_params(("arbitrary", "arbitrary")), name="fox_flash_decode",
    )(qi, kj, page_table.reshape(-1), qcat, kcat, vb, *_decode_args(pools, fox_new, mla_new, consts))


def _mla_flash_decode(layer, qcat, kcat, vcat, tq, tk, page_table, pools, fox_new, mla_new, consts, pps):
    b, _, t, _ = qcat.shape
    qi, kj = _tile_pairs(t, tq, tk)
    npairs = int(qi.shape[0])
    bs, npages = page_table.shape
    nchunk = npages // pps
    n_dec = bs * nchunk
    assert npages == nchunk * pps and b * npairs >= n_dec
    seq = lambda i, p, qi, kj, pt: jnp.minimum(i * npairs + p, n_dec - 1) // nchunk
    d_in, d_out, d_shape, d_scratch, n_new = _decode_operands(page_table, pps, seq)
    kv_spec = pl.BlockSpec((1, tk, 256), lambda i, p, qi, kj, pt: (i, kj[p], 0))
    grid_spec = pltpu.PrefetchScalarGridSpec(
        num_scalar_prefetch=3, grid=(b, npairs),
        in_specs=[pl.BlockSpec((1, H_MLA, tq, 256), lambda i, p, qi, kj, pt: (i, 0, qi[p], 0)), kv_spec,
                  kv_spec] + d_in,
        out_specs=tuple([pl.BlockSpec((1, tq, H_MLA * KV_RANK), lambda i, p, qi, kj, pt: (i, qi[p], 0))] + d_out),
        scratch_shapes=[pltpu.VMEM((H_MLA * tq, LANES), F32), pltpu.VMEM((H_MLA * tq, 2 * KV_RANK), F32)]
        + d_scratch)
    return pl.pallas_call(
        functools.partial(_mla_flash_decode_kernel, tq=tq, tk=tk, layer=layer, pps=pps, nchunk=nchunk,
                          n_dec=n_dec, n_new=n_new),
        grid_spec=grid_spec, out_shape=tuple([jax.ShapeDtypeStruct((b, t, H_MLA * KV_RANK), F32)] + d_shape),
        compiler_params=_params(("arbitrary", "arbitrary")), name="mla_flash_decode",
    )(qi, kj, page_table.reshape(-1), qcat, kcat, vcat, *_decode_args(pools, fox_new, mla_new, consts))


def _outproj_kernel(x_ref, ol_ref, of_ref, wuv_ref, gm_ref, gf_ref, wo_ref, g1_ref, b1_ref, wmq_ref,
                    x1_ref, mq_ref):
    o_mla = _mm(ol_ref[0].astype(BF16), wuv_ref[...])
    nm = _rms(o_mla, gm_ref[...]).astype(BF16)
    nf = _rms(of_ref[0], gf_ref[...]).astype(BF16)
    mix = _mm(jnp.concatenate([nm, nf], axis=1), wo_ref[...])
    x1 = _ln(ALPHA * x_ref[0] + mix, g1_ref[...], b1_ref[...])
    x1_ref[0] = x1
    mq_ref[0] = _mm(x1.astype(BF16), wmq_ref[...]) * MEM_SCALE


def _outproj(x, o_lat, o_fox, lw, tm):
    b, t, d = x.shape
    tok = lambda w: pl.BlockSpec((1, tm, w), lambda i, j: (i, j, 0))
    in_specs = [tok(d), tok(H_MLA * KV_RANK), tok(FOX_W),
                _const_spec(lw["w_uv"].shape), _const_spec((1, FOX_W)), _const_spec((1, FOX_W)),
                _const_spec(lw["w_o"].shape), _const_spec((1, d)), _const_spec((1, d)),
                _const_spec(lw["w_mq"].shape)]
    return pl.pallas_call(
        _outproj_kernel, grid=(b, t // tm), in_specs=in_specs, out_specs=(tok(d), tok(d)),
        out_shape=(jax.ShapeDtypeStruct((b, t, d), F32), jax.ShapeDtypeStruct((b, t, d), F32)),
        compiler_params=_params(("parallel", "parallel")), name="outproj",
    )(x, o_lat, o_fox, lw["w_uv"], lw["g_mla_out"], lw["g_fox_out"], lw["w_o"], lw["ln1_g"], lw["ln1_b"],
      lw["w_mq"])


def _memkv_kernel(m_ref, wk_ref, wv_ref, k_ref, v_ref):
    mb = m_ref[...].astype(BF16)
    k_ref[...] = _mm(mb, wk_ref[...])
    v_ref[...] = _mm(mb, wv_ref[...])


def _memkv(mem, wk, wv):
    n, d = mem.shape
    return pl.pallas_call(
        _memkv_kernel, grid=(1,),
        in_specs=[_const_spec((n, d)), _const_spec(wk.shape), _const_spec(wv.shape)],
        out_specs=(_const_spec((n, d)), _const_spec((n, d))),
        out_shape=(jax.ShapeDtypeStruct((n, d), F32), jax.ShapeDtypeStruct((n, d), F32)),
        compiler_params=_params(("arbitrary",)), name="memkv",
    )(mem, wk, wv)


def _cross_kernel(mq_ref, mk_ref, mv_ref, o_ref):
    mq = mq_ref[0].astype(BF16)
    outs = []
    for hd in range(H_MEM):
        mk = mk_ref[0][:, hd * D_MEM:(hd + 1) * D_MEM]
        mv = mv_ref[0][:, hd * D_MEM:(hd + 1) * D_MEM]
        s = _mm_nt(mq[:, hd * D_MEM:(hd + 1) * D_MEM], mk.astype(BF16))
        pr = jnp.exp(s - jnp.max(s, axis=1, keepdims=True))
        pr = pr / jnp.sum(pr, axis=1, keepdims=True)
        outs.append(_mm(pr.astype(BF16), mv.astype(BF16)))
    o_ref[0] = jnp.concatenate(outs, axis=1)


def _cross(mq, mk, mv, tm):
    b, t, d = mq.shape
    tok = pl.BlockSpec((1, tm, d), lambda i, j: (i, j, 0))
    mem = pl.BlockSpec((1, MEM_TOK, d), lambda i, j: (i, 0, 0))
    return pl.pallas_call(
        _cross_kernel, grid=(b, t // tm), in_specs=[tok, mem, mem], out_specs=tok,
        out_shape=jax.ShapeDtypeStruct((b, t, d), F32),
        compiler_params=_params(("parallel", "parallel")), name="cross",
    )(mq, mk, mv)


def _cross_seq_kernel(mq_ref, mk_ref, mv_ref, o_ref):
    n_q = mq_ref.shape[1]
    cols = H_MEM * n_q
    mq = mq_ref[0]
    qs = jnp.concatenate([mq[:, hd * D_MEM:(hd + 1) * D_MEM] for hd in range(H_MEM)], axis=0).astype(BF16)
    kf = mk_ref[0, 0].reshape(MEM_TOK * H_MEM, D_MEM).astype(BF16)
    vf = mv_ref[0, 0].reshape(MEM_TOK * H_MEM, D_MEM).astype(BF16)
    s3 = _mm_nt(kf, qs).reshape(MEM_TOK * H_MEM // SUBLANES, SUBLANES, cols)
    sub = lax.broadcasted_iota(jnp.int32, (SUBLANES, cols), 0)
    lane = lax.broadcasted_iota(jnp.int32, (SUBLANES, cols), 1)
    diag = sub % H_MEM == lane // n_q
    mx = jnp.max(s3, axis=0)
    mx = jnp.maximum(mx, pltpu.roll(mx, H_MEM, 0))
    ex = jnp.where(diag[None], jnp.exp(s3 - mx[None]), 0.0)
    den = jnp.sum(ex, axis=0)
    den = den + pltpu.roll(den, H_MEM, 0)
    pr = jnp.where(diag[None], ex / den[None], 0.0).reshape(MEM_TOK * H_MEM, cols)
    o = _mm_tn(pr.astype(BF16), vf)
    o_ref[0] = jnp.concatenate([o[hd * n_q:(hd + 1) * n_q] for hd in range(H_MEM)], axis=1)


def _cross_seq(mq, mk, mv, layer):
    b, t, d = mq.shape
    tok = pl.BlockSpec((1, t, d), lambda i: (i, 0, 0))
    mem = pl.BlockSpec((1, 1, MEM_TOK, H_MEM, D_MEM), lambda i: (layer, i, 0, 0, 0))
    return pl.pallas_call(
        _cross_seq_kernel, grid=(b,), in_specs=[tok, mem, mem], out_specs=tok,
        out_shape=jax.ShapeDtypeStruct((b, t, d), F32),
        compiler_params=_params(("parallel",)), name="cross_seq",
    )(mq, mk, mv)


def _ffn_kernel(*refs, seq_mode, n_chunks):
    if seq_mode:
        (x_ref, mo_ref, wmo_ref, g2_ref, b2_ref, wup_ref, cw_ref, cb_ref, wdn_ref, g3_ref, b3_ref,
         y_ref, u_ref, carry_ref) = refs
    else:
        (x_ref, mo_ref, pre_ref, wmo_ref, g2_ref, b2_ref, wup_ref, cw_ref, cb_ref, wdn_ref, g3_ref, b3_ref,
         y_ref, u_ref) = refs
    tm = x_ref.shape[1]
    dff = wdn_ref.shape[0]
    fc = dff // n_chunks

    if seq_mode:
        @pl.when(pl.program_id(1) == 0)
        def _():
            carry_ref[...] = jnp.zeros_like(carry_ref)

    x2 = _ln(ALPHA * x_ref[0] + _mm(mo_ref[0].astype(BF16), wmo_ref[...]), g2_ref[...], b2_ref[...])
    xb = x2.astype(BF16)
    row = lax.broadcasted_iota(jnp.int32, (tm, 1), 0)

    def conv(lo):
        u = _mm(xb, wup_ref[:, lo:lo + fc])
        if seq_mode:
            ext = jnp.concatenate([carry_ref[:, lo:lo + fc], u], axis=0)
            u1 = pltpu.roll(ext, 1, 0)[SUBLANES:]
            u2 = pltpu.roll(ext, 2, 0)[SUBLANES:]
            carry_ref[:, lo:lo + fc] = u[tm - SUBLANES:]
            u_ref[0, 0, :, lo:lo + fc] = u[tm - SUBLANES:]
        else:
            e = pre_ref[0, :, lo:lo + fc]
            u1 = jnp.where(row % SUBLANES < 1, pltpu.roll(e, tm - 1, 0), pltpu.roll(u, 1, 0))
            u2 = jnp.where(row % SUBLANES < 2, e, pltpu.roll(u, 2, 0))
            u_ref[0, :, lo:lo + fc] = u
        cw = cw_ref[:, lo:lo + fc]
        return cb_ref[:, lo:lo + fc] + (cw[0:1] * u2 + cw[1:2] * u1 + cw[2:3] * u)

    ffn = jnp.zeros((tm, x_ref.shape[2]), F32)
    for ci in range(n_chunks):
        val = conv(ci * fc)
        gate = conv(dff + ci * fc)
        hcat = (gate * jax.nn.sigmoid(gate) * val).astype(BF16)
        ffn = ffn + _mm(hcat, wdn_ref[ci * fc:(ci + 1) * fc, :])
    y_ref[0] = _ln(ALPHA * x2 + ffn, g3_ref[...], b3_ref[...])


def _ffn(x1, mo, lw, tm, pre=None):
    b, t, d = x1.shape
    f2 = lw["w_up"].shape[1]
    seq_mode = pre is None
    tok = lambda w: pl.BlockSpec((1, tm, w), lambda i, j: (i, j, 0))
    consts = [_const_spec(lw["w_mo"].shape), _const_spec((1, d)), _const_spec((1, d)),
              _const_spec(lw["w_up"].shape), _const_spec((3, f2)), _const_spec((1, f2)),
              _const_spec(lw["w_down"].shape), _const_spec((1, d)), _const_spec((1, d))]
    cargs = (lw["w_mo"], lw["ln2_g"], lw["ln2_b"], lw["w_up"], lw["conv_w"], lw["conv_b"], lw["w_down"],
             lw["ln3_g"], lw["ln3_b"])
    if seq_mode:
        in_specs = [tok(d), tok(d)] + consts
        args = (x1, mo) + cargs
        u_shape = jax.ShapeDtypeStruct((b, t // tm, SUBLANES, f2), F32)
        u_spec = pl.BlockSpec((1, 1, SUBLANES, f2), lambda i, j: (i, j, 0, 0))
        scratch = [pltpu.VMEM((SUBLANES, f2), F32)]
    else:
        in_specs = [tok(d), tok(d), tok(f2)] + consts
        args = (x1, mo, pre) + cargs
        u_shape = jax.ShapeDtypeStruct((b, t, f2), F32)
        u_spec = tok(f2)
        scratch = []
    return pl.pallas_call(
        functools.partial(_ffn_kernel, seq_mode=seq_mode, n_chunks=2), grid=(b, t // tm),
        in_specs=in_specs, out_specs=(tok(d), u_spec),
        out_shape=(jax.ShapeDtypeStruct((b, t, d), F32), u_shape), scratch_shapes=scratch,
        compiler_params=_params(("parallel", "arbitrary")), name="ffn",
    )(*args)


def _layer_weights(l, w_in, b_forget, g_q_lat, g_kv_lat, w_uq, w_uk, w_uv, g_mla_out, g_fox_out, w_o, ln1_g,
                   ln1_b, w_mq, w_mk, w_mv, w_mo, ln2_g, ln2_b, w_up, conv_w, conv_b, w_down, ln3_g, ln3_b):
    wi = w_in[l]
    d = wi.shape[0]
    o_kr = Q_RANK + KV_RANK
    o_fq = o_kr + ROPE
    half = ROPE // 2
    kr = wi[:, o_kr:o_fq]
    kr_sw = jnp.concatenate([kr[:, half:], kr[:, :half]], axis=1)
    n_rep = LANES // ROPE
    w_aug = jnp.concatenate([
        wi[:, :o_kr], jnp.tile(kr, (1, n_rep)), jnp.tile(kr_sw, (1, n_rep)), wi[:, o_fq:o_fq + 3 * FOX_W],
        jnp.pad(wi[:, o_fq + 3 * FOX_W:], ((0, 0), (0, LANES - H_FOX)))], axis=1).astype(BF16)
    uq = w_uq[l]
    uq_rope = uq[:, :, NOPE:]
    uq_sw = jnp.concatenate([uq_rope[:, :, half:], uq_rope[:, :, :half]], axis=2)
    w_uq_aug = jnp.concatenate([uq[:, :, :NOPE].reshape(Q_RANK, -1), uq_rope.reshape(Q_RANK, -1),
                                uq_sw.reshape(Q_RANK, -1)], axis=1).astype(BF16)
    uk_t = jnp.transpose(w_uk[l], (1, 2, 0))
    parity = jax.nn.one_hot(jnp.arange(H_MLA) % 2, 2, dtype=F32)
    w_uk_pad = (parity[:, :, None, None] * uk_t[:, None]).reshape(H_MLA, 2 * NOPE, KV_RANK).astype(BF16)
    w_uv_bd = jnp.einsum("rhv,hg->hrgv", w_uv[l], jnp.eye(H_MLA, dtype=F32)).reshape(
        H_MLA * KV_RANK, H_MLA * V_DIM).astype(BF16)
    lane = jnp.arange(LANES)
    inv = ROPE_BASE ** (-jnp.arange(half, dtype=F32) / half)
    row = lambda v: v.reshape(1, -1).astype(F32)
    return dict(
        w_in=w_aug, b_forget=row(jnp.pad(b_forget[l], (0, LANES - H_FOX))), g_q=row(g_q_lat[l]),
        g_kv=row(g_kv_lat[l]), w_uq=w_uq_aug, w_uk=w_uk_pad, inv_freq=row(inv[lane % half]),
        rope_sign=row(jnp.where(lane % ROPE < half, -1.0, 1.0)),
        w_uv=w_uv_bd, g_mla_out=row(g_mla_out[l]), g_fox_out=row(g_fox_out[l]), w_o=w_o[l].astype(BF16),
        ln1_g=row(ln1_g[l]), ln1_b=row(ln1_b[l]), w_mq=w_mq[l].reshape(d, d).astype(BF16),
        w_mk=w_mk[l].reshape(d, d).astype(BF16), w_mv=w_mv[l].reshape(d, d).astype(BF16),
        w_mo=w_mo[l].reshape(d, d).astype(BF16), ln2_g=row(ln2_g[l]), ln2_b=row(ln2_b[l]),
        w_up=w_up[l].astype(BF16), conv_w=conv_w[l].astype(F32), conv_b=row(conv_b[l]),
        w_down=w_down[l].astype(BF16), ln3_g=row(ln3_g[l]), ln3_b=row(ln3_b[l]))


def _constants(tm):
    tri = (np.arange(tm)[:, None] >= np.arange(tm)[None, :]).astype(np.float32)
    place_q = np.zeros((3, H_FOX, LANES), np.float32)
    place_k = np.zeros((3, H_FOX, LANES), np.float32)
    ones_q = np.zeros((1, LANES), np.float32)
    ones_k = np.zeros((1, LANES), np.float32)
    for hd in range(H_FOX):
        for j in range(3):
            place_q[j, hd, hd * 8 + j] = 1.0
            place_k[j, hd, hd * 8 + 3 + j] = -1.0
            ones_q[0, hd * 8 + 3 + j] = 1.0
            ones_k[0, hd * 8 + j] = 1.0
    head_mask = (np.arange(FOX_W)[None, :] // D_FOX == np.arange(H_FOX)[:, None]).astype(np.float32)
    return dict(tri=jnp.asarray(tri, BF16), place_q=jnp.asarray(place_q, BF16),
                place_k=jnp.asarray(place_k, BF16), ones_q=jnp.asarray(ones_q), ones_k=jnp.asarray(ones_k),
                triu_page=jnp.asarray(tri[:PAGE, :PAGE].T, BF16), head_mask=jnp.asarray(head_mask, BF16))


TOKEN_TILE = 512
MLA_Q_TILE = 512
MLA_Q_TILE_FUSED = 256
FFN_DEC_TILE = 256
DECODE_PAGES = 16


def _tile(n, pref):
    return pref if n % pref == 0 else n


def _layer(l, xp, xs, pos_p, pos_s, mem, caches, page_table, lw, consts):
    b, t, d = xp.shape
    tm = _tile(t, TOKEN_TILE)
    qcat, kcat, vcat, ckv, kr, fq, fk, fv, lf = _inproj(xp, pos_p, lw, tm)
    fqc, fkc, fvb = _foxprep(fq, fk, fv, lf, consts, tm)
    sample = _sample_front(xs, pos_s, lw, consts)
    pools = (caches[2], caches[3], caches[4], caches[0], caches[1])
    bs, npages = page_table.shape
    pps = _tile(npages, DECODE_PAGES)
    nchunk = npages // pps
    tq = _tile(t, MLA_Q_TILE_FUSED)
    n_fox = min(bs, b * len(_tile_pairs(t, tm, tm)[0]) // nchunk)
    n_mla = min(bs - n_fox, b * len(_tile_pairs(t, tq, tm)[0]) // nchunk)
    part = lambda lo, hi: (page_table[lo:hi], tuple(a[lo:hi] for a in sample["fox_new"]),
                           tuple(a[lo:hi] for a in sample["mla_new"]))
    dec_outs = []
    if n_fox:
        pt, fox_new, mla_new = part(0, n_fox)
        o_fox, *dec = _fox_flash_decode(l, fqc, fkc, fvb, tm, pt, pools, fox_new, mla_new, consts, pps)
        dec_outs.append(dec)
    else:
        o_fox = _fox_flash(fqc, fkc, fvb, tm)
    if n_mla:
        pt, fox_new, mla_new = part(n_fox, n_fox + n_mla)
        o_lat, *dec = _mla_flash_decode(l, qcat, kcat, vcat, tq, tm, pt, pools, fox_new, mla_new, consts, pps)
        dec_outs.append(dec)
    else:
        o_lat = _mla_flash(qcat, kcat, vcat, _tile(t, MLA_Q_TILE), tm)
    if n_fox + n_mla < bs:
        pt, fox_new, mla_new = part(n_fox + n_mla, bs)
        dec_outs.append(list(_decode(l, pt, pools, fox_new, mla_new, consts, pps)))
    o_fox_t, o_lat_s = (jnp.concatenate([d[i] for d in dec_outs], axis=0) for i in range(2))
    y_p, st_p = _prompt_back(xp, o_lat, o_fox, ckv, kr, fk, fv, lf, mem, lw, tm)
    y_s, st_s = _sample_back(l, xs, o_fox_t, o_lat_s, sample, caches, lw)
    return y_p, st_p, y_s, st_s


def _prompt_back(x, o_lat, o_fox, ckv, kr, fk, fv, lf, mem, lw, tm):
    b, t, d = x.shape
    x1, mq = _outproj(x, o_lat, o_fox, lw, tm)
    mk, mv = _memkv(mem.reshape(b * MEM_TOK, d), lw["w_mk"], lw["w_mv"])
    mk = mk.reshape(b, MEM_TOK, d)
    mv = mv.reshape(b, MEM_TOK, d)
    mo = _cross(mq, mk, mv, tm)
    y, u_tail = _ffn(x1, mo, lw, tm)
    state = (ckv, kr, fk.reshape(b, t, H_FOX, D_FOX), fv.reshape(b, t, H_FOX, D_FOX), lf,
             mk.reshape(b, MEM_TOK, H_MEM, D_MEM), mv.reshape(b, MEM_TOK, H_MEM, D_MEM),
             u_tail[:, -1, SUBLANES - 2:, :])
    return y, state


def _sample_front(x, pos, lw, consts):
    bs, ts, d = x.shape
    n = bs * ts
    tm = _tile(n, TOKEN_TILE)
    xf = x.reshape(1, n, d)
    qcat, _, _, ckv, kr, fq, fk, fv, lf = _inproj(xf, pos, lw, tm)
    q4 = qcat[0].reshape(H_MLA, bs, ts, 256)
    qa = jnp.transpose(q4[..., :KV_RANK], (1, 0, 2, 3)).reshape(bs, H_MLA * ts, KV_RANK)
    qr = jnp.stack([q4[hd, :, :, KV_RANK + (hd % 4) * ROPE:KV_RANK + (hd % 4 + 1) * ROPE]
                    for hd in range(H_MLA)], axis=1).reshape(bs, H_MLA * ts, ROPE)
    keys_last = lambda a: jnp.pad(jnp.swapaxes(a.reshape(bs, ts, -1), 1, 2), ((0, 0), (0, 0), (0, PAGE - ts)))
    ckn = jnp.pad(ckv.reshape(bs, ts, KV_RANK), ((0, 0), (0, PAGE - ts), (0, 0)))
    qbd = (fq.reshape(bs, ts, 1, FOX_W) * consts["head_mask"][None, None]).reshape(bs, ts * H_FOX, FOX_W)
    return dict(fox_new=(qbd, keys_last(fk), keys_last(fv), keys_last(lf)), mla_new=(qa, qr, ckn, keys_last(kr)),
                ckv=ckv, kr=kr, fk=fk, fv=fv, lf=lf)


def _sample_back(l, x, o_fox_t, o_lat, sample, caches, lw):
    bs, ts, d = x.shape
    n = bs * ts
    tm = _tile(n, TOKEN_TILE)
    xf = x.reshape(1, n, d)
    c_mk, c_mv, c_conv = caches[5:]
    ckv, kr, fk, fv, lf = (sample[k] for k in ("ckv", "kr", "fk", "fv", "lf"))
    o_lat = jnp.transpose(o_lat.reshape(bs, H_MLA, ts, KV_RANK), (0, 2, 1, 3)).reshape(1, n, H_MLA * KV_RANK)
    o_fox = jnp.transpose(o_fox_t.reshape(bs, D_FOX, ts, H_FOX), (0, 2, 3, 1)).reshape(1, n, FOX_W)
    fk4 = fk.reshape(bs, ts, H_FOX, D_FOX)
    fv4 = fv.reshape(bs, ts, H_FOX, D_FOX)
    lf3 = lf.reshape(bs, ts, H_FOX)
    x1, mq = _outproj(xf, o_lat, o_fox, lw, tm)
    mo = _cross_seq(mq.reshape(bs, ts, d), c_mk, c_mv, l).reshape(1, n, d)
    pre = jnp.pad(c_conv[l], ((0, 0), (0, ts - c_conv.shape[2]), (0, 0))).reshape(1, n, -1)
    y, u = _ffn(x1, mo, lw, _tile(n, FFN_DEC_TILE), pre=pre)
    state = (ckv.reshape(bs, ts, KV_RANK), kr.reshape(bs, ts, ROPE), fk4, fv4, lf3,
             u.reshape(bs, ts, -1)[:, ts - 2:, :])
    return y.reshape(bs, ts, d), state


def kernel(x_prompt, x_sample, mem_prompt, cache_mla_ckv, cache_mla_krope, cache_fox_k, cache_fox_v, cache_fox_logf, cache_mem_k, cache_mem_v, state_conv, page_table, w_in, b_forget, g_q_lat, g_kv_lat, w_uq, w_uk, w_uv, g_mla_out, g_fox_out, w_o, ln1_g, ln1_b, w_mq, w_mk, w_mv, w_mo, ln2_g, ln2_b, w_up, conv_w, conv_b, w_down, ln3_g, ln3_b):
    bp, tp, _ = x_prompt.shape
    bs, ts, _ = x_sample.shape
    n_past = page_table.shape[1] * PAGE
    pos_p = jnp.broadcast_to(jnp.arange(tp, dtype=F32)[None, :, None], (bp, tp, 1))
    pos_s = (n_past + jnp.arange(bs * ts) % ts).astype(F32).reshape(1, bs * ts, 1)
    consts = _constants(_tile(tp, TOKEN_TILE))
    n_pool = cache_fox_k.shape[1]
    kv_view = lambda a: jnp.transpose(a, (0, 1, 3, 4, 2)).reshape(DEPTH, n_pool, FOX_W, PAGE)
    caches = (cache_mla_ckv, jnp.swapaxes(cache_mla_krope, 2, 3), kv_view(cache_fox_k), kv_view(cache_fox_v),
              jnp.swapaxes(cache_fox_logf, 2, 3), cache_mem_k, cache_mem_v, state_conv)
    weights = (w_in, b_forget, g_q_lat, g_kv_lat, w_uq, w_uk, w_uv, g_mla_out, g_fox_out, w_o, ln1_g, ln1_b,
               w_mq, w_mk, w_mv, w_mo, ln2_g, ln2_b, w_up, conv_w, conv_b, w_down, ln3_g, ln3_b)
    xp, xs = x_prompt, x_sample
    p_states, s_states = [], []
    for l in range(DEPTH):
        lw = _layer_weights(l, *weights)
        xp, st_p, xs, st_s = _layer(l, xp, xs, pos_p, pos_s, mem_prompt, caches, page_table, lw, consts)
        p_states.append(st_p)
        s_states.append(st_s)
    stack = lambda states, i: jnp.stack([st[i] for st in states])
    return ((xp, xs) + tuple(stack(p_states, i) for i in range(8)) + tuple(stack(s_states, i) for i in range(6)))
```
